```python
import math
import jax, jax.numpy as jnp
from jax import lax
import numpy as np

D_MODEL = 1024
BATCH = 4
SEQ = 8192
DEPTH = 1

N_META = 16
RWKV_WIDTH = D_MODEL // 2
HEAD_DIM = 64
RWKV_HEADS = RWKV_WIDTH // HEAD_DIM
W_LORA = 64
A_LORA = 64
G_LORA = 128
RWKV_PROJ = 3 * RWKV_WIDTH + W_LORA + A_LORA + G_LORA
RWKV_SPLITS = [RWKV_WIDTH, 2 * RWKV_WIDTH, 3 * RWKV_WIDTH, 3 * RWKV_WIDTH + W_LORA, 3 * RWKV_WIDTH + W_LORA + A_LORA]
GN_EPS = 64e-5
S5_WIDTH = D_MODEL // 2
S5_GROUP = 16
S5_GROUPS = S5_WIDTH // S5_GROUP
S5_STATE = 64
N_BRANCH = 2
PROJ_WIDTH = RWKV_PROJ + S5_WIDTH + N_BRANCH * D_MODEL
D_FF = 2816
CONV_W = 3
RMS_EPS = 1e-6

kernel_name = "rwkv7_s5_gated_hybrid_block"


def _rms_norm(x, g):
    xf = x.astype(jnp.float32)
    y = xf * lax.rsqrt(jnp.mean(xf * xf, axis=-1, keepdims=True) + RMS_EPS)
    return (y * g.astype(jnp.float32)).astype(x.dtype)


def _token_shift(p):
    return jnp.pad(p, ((0, 0), (1, 0), (0, 0)))[:, :-1, :]


def _rwkv7_step(S, inp):
    r_t, w_t, k_t, v_t, kk_t, b_t = inp
    sa = jnp.einsum('bhvk,bhk->bhv', S, -kk_t)
    S = S * w_t[:, :, None, :] + sa[..., None] * b_t[:, :, None, :] + v_t[..., None] * k_t[:, :, None, :]
    y = jnp.einsum('bhvk,bhk->bhv', S, r_t)
    return S, y


def _rwkv7_branch(p, mu, w0, w_up, a0, a_up, g_up, k_k, k_a, r_k, ln_g, ln_b, w_branch):
    f32 = jnp.float32
    bsz, t_len, _ = p.shape
    p = p + (_token_shift(p) - p) * mu
    r, k, v, wd, ad, gd = jnp.split(p, RWKV_SPLITS, axis=-1)
    w_log = -jax.nn.softplus(-(w0 + jnp.tanh(wd) @ w_up)) - 0.5
    decay = jnp.exp(-jnp.exp(w_log.astype(f32)))
    a = jax.nn.sigmoid(a0 + ad @ a_up)
    g = jax.nn.sigmoid(gd) @ g_up

    def heads(t):
        return t.astype(f32).reshape(bsz, t_len, RWKV_HEADS, HEAD_DIM)

    kk = heads(k * k_k)
    kk = kk / jnp.maximum(jnp.linalg.norm(kk, axis=-1, keepdims=True), 1e-12)
    k = k * (1 + (a - 1) * k_a)
    r_h, k_h, v_h, w_h, a_h = heads(r), heads(k), heads(v), heads(decay), heads(a)

    def tm(t):
        return jnp.moveaxis(t, 1, 0)

    s0 = jnp.zeros((bsz, RWKV_HEADS, HEAD_DIM, HEAD_DIM), f32)
    _, y = lax.scan(_rwkv7_step, s0, (tm(r_h), tm(w_h), tm(k_h), tm(v_h), tm(kk), tm(kk * a_h)))
    y = jnp.moveaxis(y, 0, 1)
    mean = jnp.mean(y, axis=-1, keepdims=True)
    var = jnp.mean(jnp.square(y - mean), axis=-1, keepdims=True)
    y = ((y - mean) * lax.rsqrt(var + GN_EPS)).reshape(bsz, t_len, RWKV_WIDTH)
    y = y * ln_g.astype(f32) + ln_b.astype(f32)
    bonus = jnp.sum(r_h * k_h * r_k.astype(f32), axis=-1, keepdims=True) * v_h
    y = (y + bonus.reshape(bsz, t_len, RWKV_WIDTH)) * g.astype(f32)
    return y.astype(p.dtype) @ w_branch


def _ssm_combine(e1, e2):
    a1, b1 = e1
    a2, b2 = e2
    return a1 * a2, a2 * b1 + b2


def _s5_branch(u, lam_re, lam_im, log_dt, b_re, b_im, c_re, c_im, d, w_glu):
    f32 = jnp.float32
    bsz, t_len, _ = u.shape
    ut = jnp.moveaxis(u.astype(f32).reshape(bsz, t_len, S5_GROUPS, S5_GROUP), 1, 0)
    lam = lax.complex(jnp.minimum(lam_re.astype(f32), -1e-4), lam_im.astype(f32))
    dt = jnp.exp(log_dt.astype(f32))[:, None]
    lam_bar = jnp.exp(lam * dt)
    b_bar = ((lam_bar - 1) / lam)[..., None] * lax.complex(b_re.astype(f32), b_im.astype(f32))
    bu = jnp.einsum('tbgh,gnh->tbgn', ut.astype(jnp.complex64), b_bar)
    a_el = jnp.broadcast_to(lam_bar, (t_len, 1) + lam_bar.shape)
    _, xs = lax.associative_scan(_ssm_combine, (a_el, bu), axis=0)
    c = lax.complex(c_re.astype(f32), c_im.astype(f32))
    y = jnp.einsum('tbgn,ghn->tbgh', xs, c).real + d.astype(f32).reshape(S5_GROUPS, S5_GROUP) * ut
    y = jnp.moveaxis(y, 0, 1).reshape(bsz, t_len, S5_WIDTH)
    y = jax.nn.gelu(y).astype(u.dtype)
    z = y @ w_glu
    return z[..., :D_MODEL] * jax.nn.sigmoid(z[..., D_MODEL:])


def _causal_dwconv(a, w, b):
    c = w.shape[-1]
    y = lax.conv_general_dilated(a, w[:, None, :].astype(a.dtype), window_strides=(1,),
                                 padding=[(CONV_W - 1, 0)], dimension_numbers=('NWC', 'WIO', 'NWC'),
                                 feature_group_count=c)
    return y + b


def setup_inputs(seed: int = 0) -> dict:
    key = jax.random.key(seed)
    ks = iter(jax.random.split(key, 40))
    L = DEPTH

    def nrm(shape, scale):
        return jax.random.normal(next(ks), shape, jnp.float32) * scale

    def unif(shape, lo, hi):
        return jax.random.uniform(next(ks), shape, jnp.float32, lo, hi)

    lam_im0 = jnp.broadcast_to(math.pi * jnp.arange(S5_STATE, dtype=jnp.float32), (L, S5_GROUPS, S5_STATE))
    return {
        "x": nrm((BATCH, SEQ, D_MODEL), 1.0),
        "meta_tokens": nrm((N_META, D_MODEL), 1.0),
        "norm_mix_pre": 1.0 + nrm((L, D_MODEL), 0.02),
        "norm_mix_post": 1.0 + nrm((L, D_MODEL), 0.02),
        "w_in": nrm((L, D_MODEL, PROJ_WIDTH), D_MODEL ** -0.5),
        "rwkv_shift_mu": unif((L, RWKV_PROJ), 0.0, 1.0),
        "rwkv_w0": unif((L, RWKV_WIDTH), -6.0, 1.0),
        "rwkv_w_up": nrm((L, W_LORA, RWKV_WIDTH), 0.1),
        "rwkv_a0": nrm((L, RWKV_WIDTH), 0.1),
        "rwkv_a_up": nrm((L, A_LORA, RWKV_WIDTH), 0.1),
        "rwkv_g_up": nrm((L, G_LORA, RWKV_WIDTH), G_LORA ** -0.5),
        "rwkv_k_k": 0.85 + nrm((L, RWKV_WIDTH), 0.02),
        "rwkv_k_a": 1.0 + nrm((L, RWKV_WIDTH), 0.02),
        "rwkv_r_k": nrm((L, RWKV_HEADS, HEAD_DIM), 0.1),
        "rwkv_ln_gain": 1.0 + nrm((L, RWKV_WIDTH), 0.02),
        "rwkv_ln_bias": nrm((L, RWKV_WIDTH), 0.02),
        "rwkv_w_branch": nrm((L, RWKV_WIDTH, D_MODEL), RWKV_WIDTH ** -0.5),
        "s5_lambda_re": -0.5 + nrm((L, S5_GROUPS, S5_STATE), 0.01),
        "s5_lambda_im": lam_im0 + nrm((L, S5_GROUPS, S5_STATE), 0.01),
        "s5_log_dt": unif((L, S5_GROUPS), math.log(1e-3), math.log(1e-1)),
        "s5_b_re": nrm((L, S5_GROUPS, S5_STATE, S5_GROUP), (2 * S5_GROUP) ** -0.5),
        "s5_b_im": nrm((L, S5_GROUPS, S5_STATE, S5_GROUP), (2 * S5_GROUP) ** -0.5),
        "s5_c_re": nrm((L, S5_GROUPS, S5_GROUP, S5_STATE), (2 * S5_STATE) ** -0.5),
        "s5_c_im": nrm((L, S5_GROUPS, S5_GROUP, S5_STATE), (2 * S5_STATE) ** -0.5),
        "s5_d": nrm((L, S5_WIDTH), 1.0),
        "s5_w_glu": nrm((L, S5_WIDTH, 2 * D_MODEL), S5_WIDTH ** -0.5),
        "w_out": nrm((L, D_MODEL, D_MODEL), D_MODEL ** -0.5),
        "norm_ffn_pre": 1.0 + nrm((L, D_MODEL), 0.02),
        "norm_ffn_post": 1.0 + nrm((L, D_MODEL), 0.02),
        "ffn_w_up": nrm((L, D_MODEL, 2 * D_FF), D_MODEL ** -0.5),
        "ffn_conv_w": nrm((L, CONV_W, D_FF), CONV_W ** -0.5),
        "ffn_conv_b": nrm((L, D_FF), 0.02),
        "ffn_w_down": nrm((L, D_FF, D_MODEL), D_FF ** -0.5),
    }


def reference(x, meta_tokens, norm_mix_pre, norm_mix_post, w_in, rwkv_shift_mu, rwkv_w0, rwkv_w_up,
              rwkv_a0, rwkv_a_up, rwkv_g_up, rwkv_k_k, rwkv_k_a, rwkv_r_k, rwkv_ln_gain, rwkv_ln_bias,
              rwkv_w_branch, s5_lambda_re, s5_lambda_im, s5_log_dt, s5_b_re, s5_b_im, s5_c_re, s5_c_im,
              s5_d, s5_w_glu, w_out, norm_ffn_pre, norm_ffn_post, ffn_w_up, ffn_conv_w, ffn_conv_b,
              ffn_w_down):
    bsz = x.shape[0]
    meta = jnp.broadcast_to(meta_tokens[None].astype(x.dtype), (bsz, N_META, D_MODEL))
    h = jnp.concatenate([meta, x], axis=1)
    for l in range(DEPTH):
        hn = _rms_norm(h, norm_mix_pre[l])
        proj = hn @ w_in[l]
        p_rwkv = proj[..., :RWKV_PROJ]
        u_s5 = proj[..., RWKV_PROJ:RWKV_PROJ + S5_WIDTH]
        gates = jax.nn.sigmoid(proj[..., RWKV_PROJ + S5_WIDTH:])
        y_a = _rwkv7_branch(p_rwkv, rwkv_shift_mu[l], rwkv_w0[l], rwkv_w_up[l], rwkv_a0[l], rwkv_a_up[l],
                            rwkv_g_up[l], rwkv_k_k[l], rwkv_k_a[l], rwkv_r_k[l], rwkv_ln_gain[l],
                            rwkv_ln_bias[l], rwkv_w_branch[l])
        y_b = _s5_branch(u_s5, s5_lambda_re[l], s5_lambda_im[l], s5_log_dt[l], s5_b_re[l], s5_b_im[l],
                         s5_c_re[l], s5_c_im[l], s5_d[l], s5_w_glu[l])
        mix = (gates[..., :D_MODEL] * y_a + gates[..., D_MODEL:] * y_b) @ w_out[l]
        h = h + _rms_norm(mix, norm_mix_post[l])
        hn = _rms_norm(h, norm_ffn_pre[l])
        up = hn @ ffn_w_up[l]
        a = _causal_dwconv(up[..., :D_FF], ffn_conv_w[l], ffn_conv_b[l])
        f = (jax.nn.gelu(a, approximate=True) * up[..., D_FF:]) @ ffn_w_down[l]
        h = h + _rms_norm(f, norm_ffn_post[l])
    return h[:, N_META:, :]
```

```python
import functools
import math

import jax
import jax.numpy as jnp
from jax import lax
from jax.experimental import pallas as pl
from jax.experimental.pallas import tpu as pltpu

F32 = jnp.float32
BF16 = jnp.bfloat16
HI = lax.Precision.HIGHEST

D_MODEL = 1024
N_META = 16
RWKV_WIDTH = 512
HEAD_DIM = 64
W_LORA = 64
A_LORA = 64
G_LORA = 128
RWKV_PROJ = 3 * RWKV_WIDTH + W_LORA + A_LORA + G_LORA
GN_EPS = 64e-5
S5_WIDTH = 512
S5_GROUP = 16
S5_GROUPS = 32
S5_STATE = 64
D_FF = 2816
RMS_EPS = 1e-6

LANES = 128
CHUNK = 64
CHUNK_GROUP = 2
S5_TOK = 8
S5_PAIRS = S5_GROUPS // 2
S5_COLS = S5_TOK * S5_WIDTH
S5_NSTATE = S5_GROUPS * S5_STATE
META_PAD = 64
VMEM_LIMIT = 56 * 1024 * 1024


def _dot(a, b, precision=None):
    return jnp.dot(a, b, preferred_element_type=F32, precision=precision)


def _dot_nt(a, b):
    return lax.dot_general(a, b, (((1,), (1,)), ((), ())), preferred_element_type=F32)


def _dot_tn(a, b):
    return lax.dot_general(a, b, (((0,), (0,)), ((), ())), preferred_element_type=F32)


def _bf(x):
    return x.astype(BF16)


def _split(x):
    hi = x.astype(BF16)
    return hi, (x - hi.astype(F32)).astype(BF16)


def _dot_x3(a, b):
    ah, al = _split(a)
    bh, bl = _split(b)
    return _dot(ah, bh) + _dot(al, bh) + _dot(ah, bl)


def _gelu_tanh(x):
    c = math.sqrt(2.0 / math.pi)
    return 0.5 * x * (1.0 + jnp.tanh(c * (x + 0.044715 * (x * x * x))))


def _softplus(x):
    return jnp.maximum(x, 0.0) + jnp.log(1.0 + jnp.exp(-jnp.abs(x)))


def _rms(x, g):
    ms = jnp.mean(x * x, axis=-1, keepdims=True)
    return x * lax.rsqrt(ms + RMS_EPS) * g


def _shift_rows(x, s, row):
    return jnp.where(row >= s, pltpu.roll(x, s, 0), 0.0)


def _params(*sem):
    return pltpu.CompilerParams(dimension_semantics=sem, vmem_limit_bytes=VMEM_LIMIT)


def _lane_blocks(tm):
    return pl.BlockSpec((S5_WIDTH // LANES, None, tm, LANES), lambda b, i: (0, b, i, 0))


def _inproj_kernel(h_ref, g_ref, w_ref, p_ref, u_ref, gate_ref):
    hn = _rms(h_ref[...], g_ref[...]).astype(BF16)
    p_ref[...] = _dot(hn, w_ref[:, 0:RWKV_PROJ])
    u = _dot(hn, w_ref[:, RWKV_PROJ:RWKV_PROJ + S5_WIDTH])
    for blk in range(S5_WIDTH // LANES):
        u_ref[blk] = u[:, blk * LANES:(blk + 1) * LANES]
    gate_ref[...] = jax.nn.sigmoid(_dot(hn, w_ref[:, RWKV_PROJ + S5_WIDTH:])).astype(BF16)


def _inproj(h, g, w, tm):
    bsz, t_len, _ = h.shape
    n_w = w.shape[1]
    row = lambda c: pl.BlockSpec((None, tm, c), lambda b, i: (b, i, 0))
    full = lambda a: pl.BlockSpec(a.shape, lambda b, i: (0,) * a.ndim)
    return pl.pallas_call(
        _inproj_kernel,
        grid=(bsz, t_len // tm),
        in_specs=[row(D_MODEL), full(g), full(w)],
        out_specs=[row(RWKV_PROJ), _lane_blocks(tm), row(n_w - RWKV_PROJ - S5_WIDTH)],
        out_shape=[jax.ShapeDtypeStruct((bsz, t_len, RWKV_PROJ), F32),
                   jax.ShapeDtypeStruct((S5_WIDTH // LANES, bsz, t_len, LANES), F32),
                   jax.ShapeDtypeStruct((bsz, t_len, n_w - RWKV_PROJ - S5_WIDTH), BF16)],
        compiler_params=_params("parallel", "parallel"),
        name="inproj",
    )(h, g, w)


def _rwkv_kernel(p_ref, prev_ref, sin_ref, mu_ref, w0_ref, wup_ref, a0_ref, aup_ref, gup_ref,
                 kk_ref, ka_ref, rk_ref, lng_ref, lnb_ref,
                 y_ref, sout_ref,
                 prev_sc, state_sc, at_sc, rt_sc, bh_sc, kh_sc, bl_sc, kl_sc, v_sc, gl_sc,
                 bon_sc, g_sc, rp_sc, yl_sc, phi_sc, psi_sc, *, tm):
    L = CHUNK
    n_chunks = tm // L
    n_pairs = RWKV_WIDTH // LANES
    n2 = 2 * L

    @pl.when(pl.program_id(1) == 0)
    def _():
        prev_sc[...] = prev_ref[...]
        state_sc[...] = sin_ref[...]

    si = lax.broadcasted_iota(jnp.int32, (n2, n2), 0)
    sj = lax.broadcasted_iota(jnp.int32, (n2, n2), 1)
    same_head = (si // L) == (sj // L)
    strict = same_head & ((si % L) > (sj % L))
    incl = same_head & ((si % L) >= (sj % L))
    eye_mask = si == sj
    eye = eye_mask.astype(F32)
    head_sum = same_head.astype(BF16)
    head0 = lax.broadcasted_iota(jnp.int32, (L, LANES), 1) < HEAD_DIM

    p = p_ref[...]
    row = lax.broadcasted_iota(jnp.int32, p.shape, 0)
    ps = jnp.where(row == 0, prev_sc[...], pltpu.roll(p, 1, 0))
    prev_sc[...] = p_ref[pl.ds(tm - 1, 1), :]
    pm = p + (ps - p) * mu_ref[...]
    r = pm[:, 0:RWKV_WIDTH]
    k = pm[:, RWKV_WIDTH:2 * RWKV_WIDTH]
    v = pm[:, 2 * RWKV_WIDTH:3 * RWKV_WIDTH]
    wa = pm[:, 3 * RWKV_WIDTH:3 * RWKV_WIDTH + W_LORA + A_LORA]
    gd = pm[:, 3 * RWKV_WIDTH + W_LORA + A_LORA:]

    w_log = -_softplus(-(w0_ref[...] + _dot_x3(jnp.tanh(wa), wup_ref[...]))) - 0.5
    logw = -jnp.exp(w_log)
    a = jax.nn.sigmoid(a0_ref[...] + _dot_x3(wa, aup_ref[...]))
    g_sc[...] = _dot(_bf(jax.nn.sigmoid(gd)), _bf(gup_ref[...]))
    kx = k * kk_ref[...]
    kx2 = kx * kx
    rk = r * k
    v_sc[...] = v

    li = lax.broadcasted_iota(jnp.int32, (L, L), 0)
    lj = lax.broadcasted_iota(jnp.int32, (L, L), 1)
    ltri = (li >= lj).astype(BF16)
    for j in range(n_pairs):
        cols = slice(j * LANES, (j + 1) * LANES)
        a_j = a[:, cols]
        kkn = kx[:, cols] / jnp.maximum(jnp.sqrt(_dot(_bf(kx2[:, cols]), head_sum)), 1e-12)
        scale = 1.0 + (a_j - 1.0) * ka_ref[:, cols]
        kp = k[:, cols] * scale
        b = kkn * a_j
        bon_sc[:, cols] = _dot(_bf(rk[:, cols] * scale * rk_ref[:, cols]), head_sum) * v[:, cols]
        for c in range(n_chunks):
            rows = slice(c * L, (c + 1) * L)
            lw = logw[rows, cols]
            lw_hi, lw_lo = _split(lw)
            cum = _dot(ltri, lw_hi) + _dot(ltri, lw_lo)
            cum_l = jnp.sum(lw, axis=0, keepdims=True)
            at_sc[rows, cols] = -kkn[rows] * jnp.exp(cum - lw)
            rt_sc[rows, cols] = r[rows, cols] * jnp.exp(cum)
            ginv = jnp.exp(-cum)
            bh_sc[rows, cols] = b[rows] * ginv
            kh_sc[rows, cols] = kp[rows] * ginv
            tail = jnp.exp(cum_l - cum)
            bl_sc[rows, cols] = b[rows] * tail
            kl_sc[rows, cols] = kp[rows] * tail
            gl_sc[c * 8:(c + 1) * 8, cols] = jnp.broadcast_to(jnp.exp(cum_l), (8, LANES))

    def split_heads(x):
        return jnp.concatenate([jnp.where(head0, x, 0.0), jnp.where(head0, 0.0, x)], axis=0)

    def pick_heads(xs):
        return jnp.where(head0, xs[:L], xs[L:])

    def merge_heads(xs):
        return xs[:L] + xs[L:]

    group = min(CHUNK_GROUP, n_chunks)

    def local_body(cg, carry):
        units = [(cg * group + dc, j) for dc in range(group) for j in range(n_pairs)]
        n_u = range(len(units))
        rws = [pl.ds(pl.multiple_of(c * L, L), L) for c, _ in units]
        cl = [slice(j * LANES, (j + 1) * LANES) for _, j in units]
        at_s = [split_heads(at_sc[rws[i], cl[i]]) for i in n_u]
        rt_s = [split_heads(rt_sc[rws[i], cl[i]]) for i in n_u]
        bl = [_bf(bl_sc[rws[i], cl[i]]) for i in n_u]
        kl = [_bf(kl_sc[rws[i], cl[i]]) for i in n_u]
        vv = [v_sc[rws[i], cl[i]] for i in n_u]
        gl = [gl_sc[pl.ds(pl.multiple_of(units[i][0] * 8, 8), 1), cl[i]] for i in n_u]
        bh_s = [_bf(split_heads(bh_sc[rws[i], cl[i]])) for i in n_u]
        kh_s = [_bf(split_heads(kh_sc[rws[i], cl[i]])) for i in n_u]
        at_b = [_bf(x) for x in at_s]
        lhs = [jnp.concatenate([at_b[i], _bf(rt_s[i])], axis=0) for i in n_u]
        ab = [_dot_nt(lhs[i], bh_s[i]) for i in n_u]
        ak = [_dot_nt(lhs[i], kh_s[i]) for i in n_u]
        a_ab = [jnp.where(strict, ab[i][:n2], 0.0) for i in n_u]
        a_ak = [_bf(jnp.where(strict, ak[i][:n2], 0.0)) for i in n_u]
        a_rb = [_bf(jnp.where(incl, ab[i][n2:], 0.0)) for i in n_u]
        a_rk = [_bf(jnp.where(incl, ak[i][n2:], 0.0)) for i in n_u]
        tinv = [eye for _ in n_u]
        s = 1
        while s < L:
            off = ((si // (2 * s)) == (sj // (2 * s))) & ((si % (2 * s)) >= s) & ((sj % (2 * s)) < s)
            t_b = [_bf(t) for t in tinv]
            x1 = [_dot(_bf(jnp.where(off, a_ab[i], 0.0)), t_b[i]) for i in n_u]
            x2 = [_dot(t_b[i], _bf(x1[i])) for i in n_u]
            tinv = [tinv[i] + x2[i] for i in n_u]
            s *= 2
        v2 = [_bf(jnp.concatenate([vv[i], vv[i]], axis=0)) for i in n_u]
        av = [_dot(a_ak[i], v2[i]) for i in n_u]
        tx = [_dot(_bf(tinv[i]), jnp.concatenate([at_b[i], _bf(av[i])], axis=1)) for i in n_u]
        g1 = [_dot(a_rb[i], _bf(tx[i])) for i in n_u]
        g2 = [_dot(a_rk[i], v2[i]) for i in n_u]
        w = [_bf(merge_heads(tx[i][:, :LANES])) for i in n_u]
        uv = [_bf(jnp.concatenate([pick_heads(tx[i][:, LANES:]), vv[i]], axis=0)) for i in n_u]
        phi = [_dot_tn(w[i], bl[i]) for i in n_u]
        psi = [_dot_tn(uv[i], jnp.concatenate([bl[i], kl[i]], axis=0)) for i in n_u]
        for i in n_u:
            c, j = units[i]
            rp_sc[rws[i], cl[i]] = merge_heads(rt_s[i] + g1[i][:, :LANES])
            yl_sc[rws[i], cl[i]] = pick_heads(g1[i][:, LANES:] + g2[i])
            phi_sc[c, j] = jnp.where(same_head, phi[i], 0.0) + jnp.where(eye_mask, gl[i], 0.0)
            psi_sc[c, j] = jnp.where(same_head, psi[i], 0.0)
        return carry

    lax.fori_loop(0, n_chunks // group, local_body, 0)

    def state_body(c, carry):
        rws = pl.ds(pl.multiple_of(c * L, L), L)
        pairs = range(n_pairs)
        cl = [slice(j * LANES, (j + 1) * LANES) for j in pairs]
        st = [_split(state_sc[j]) for j in pairs]
        ph = [_split(phi_sc[c, j]) for j in pairs]
        rp = [_bf(rp_sc[rws, cl[j]]) for j in pairs]
        y0 = [_dot_nt(rp[j], st[j][0]) for j in pairs]
        n0 = [_dot(st[j][0], ph[j][0]) for j in pairs]
        n1 = [_dot(st[j][1], ph[j][0]) for j in pairs]
        n2_ = [_dot(st[j][0], ph[j][1]) for j in pairs]
        for j in pairs:
            rp_sc[rws, cl[j]] = y0[j] + yl_sc[rws, cl[j]]
            state_sc[j] = n0[j] + n1[j] + n2_[j] + psi_sc[c, j]
        return carry

    lax.fori_loop(0, n_chunks, state_body, 0)

    inv_n = 1.0 / HEAD_DIM
    for j in range(n_pairs):
        cols = slice(j * LANES, (j + 1) * LANES)
        y = rp_sc[:, cols]
        yc = y - _dot(_bf(y), head_sum) * inv_n
        var = _dot(_bf(yc * yc), head_sum) * inv_n
        yn = yc * lax.rsqrt(var + GN_EPS) * lng_ref[:, cols] + lnb_ref[:, cols]
        y_ref[:, cols] = ((yn + bon_sc[:, cols]) * g_sc[:, cols]).astype(BF16)
    sout_ref[...] = state_sc[...]


def _rwkv(p, prev_row, state_in, prm, tm):
    bsz, t_len, _ = p.shape
    n_pairs = RWKV_WIDTH // LANES
    n_chunks = tm // CHUNK
    full = lambda a: pl.BlockSpec(a.shape, lambda b, i: (0,) * a.ndim)
    weights = [prm["mu"], prm["w0"], prm["wup"], prm["a0"], prm["aup"], prm["gup"], prm["k_k"], prm["k_a"],
               prm["r_k"], prm["ln_g"], prm["ln_b"]]
    tile = pltpu.VMEM((tm, RWKV_WIDTH), F32)
    mats = pltpu.VMEM((n_chunks, n_pairs, LANES, LANES), F32)
    return pl.pallas_call(
        functools.partial(_rwkv_kernel, tm=tm),
        grid=(bsz, t_len // tm),
        in_specs=[pl.BlockSpec((None, tm, RWKV_PROJ), lambda b, i: (b, i, 0)), full(prev_row), full(state_in)]
                 + [full(a) for a in weights],
        out_specs=[pl.BlockSpec((None, tm, RWKV_WIDTH), lambda b, i: (b, i, 0)),
                   pl.BlockSpec((None, n_pairs, LANES, LANES), lambda b, i: (b, 0, 0, 0))],
        out_shape=[jax.ShapeDtypeStruct((bsz, t_len, RWKV_WIDTH), BF16),
                   jax.ShapeDtypeStruct((bsz, n_pairs, LANES, LANES), F32)],
        scratch_shapes=[pltpu.VMEM((1, RWKV_PROJ), F32), pltpu.VMEM((n_pairs, LANES, LANES), F32)]
                       + [tile] * 7 + [pltpu.VMEM((n_chunks * 8, RWKV_WIDTH), F32)] + [tile] * 4 + [mats] * 2,
        compiler_params=_params("parallel", "arbitrary"),
        name="rwkv",
    )(p, prev_row, state_in, *weights)


def _s5_kernel(u_ref, sin_ref, m_ref, qr_ref, qi_ref, pr_ref, pi_ref, lr_ref, li_ref, d_ref,
               y_ref, sout_ref,
               st_sc, ug_sc, zr_sc, zi_sc, ya_sc, *, rows):
    @pl.when(pl.program_id(1) == 0)
    def _():
        st_sc[...] = sin_ref[...]

    lane_t = lax.broadcasted_iota(jnp.int32, (rows, LANES), 1) // S5_GROUP
    groups_per_vreg = LANES // S5_GROUP

    for g in range(S5_GROUPS):
        blk, gl = divmod(g, groups_per_vreg)
        acc = jnp.zeros((rows, LANES), F32)
        for t in range(S5_TOK):
            src = u_ref[blk, pl.ds(t, rows, stride=S5_TOK), :]
            sh = (S5_GROUP * (t - gl)) % LANES
            if sh:
                src = pltpu.roll(src, sh, 1)
            acc = jnp.where(lane_t == t, src, acc)
        ug_sc[:, g * LANES:(g + 1) * LANES] = acc

    for q in range(S5_PAIRS):
        u2 = ug_sc[:, q * 2 * LANES:(q + 1) * 2 * LANES].astype(BF16)
        zr_sc[:, q * LANES:(q + 1) * LANES] = _dot(u2, qr_ref[q])
        zi_sc[:, q * LANES:(q + 1) * LANES] = _dot(u2, qi_ref[q])

    row = lax.broadcasted_iota(jnp.int32, (rows, S5_NSTATE), 0)
    sr = st_sc[0:1, :]
    si = st_sc[1:2, :]
    l8r = lr_ref[0:1, :]
    l8i = li_ref[0:1, :]
    xr = zr_sc[...] + jnp.where(row == 0, l8r * sr - l8i * si, 0.0)
    xi = zi_sc[...] + jnp.where(row == 0, l8r * si + l8i * sr, 0.0)
    s = 1
    lvl = 0
    while s < rows:
        ar = lr_ref[lvl:lvl + 1, :]
        ai = li_ref[lvl:lvl + 1, :]
        pr = _shift_rows(xr, s, row)
        pi = _shift_rows(xi, s, row)
        xr, xi = xr + (ar * pr - ai * pi), xi + (ar * pi + ai * pr)
        s *= 2
        lvl += 1
    zr_sc[...] = xr
    zi_sc[...] = xi
    st_sc[0:1, :] = zr_sc[rows - 1:rows, :]
    st_sc[1:2, :] = zi_sc[rows - 1:rows, :]
    sout_ref[...] = st_sc[...]
    x0r = jnp.where(row == 0, sr, pltpu.roll(xr, 1, 0))
    x0i = jnp.where(row == 0, si, pltpu.roll(xi, 1, 0))
    zr_sc[...] = x0r
    zi_sc[...] = x0i

    for q in range(S5_PAIRS):
        cols2 = slice(q * 2 * LANES, (q + 1) * 2 * LANES)
        cols1 = slice(q * LANES, (q + 1) * LANES)
        u2 = ug_sc[:, cols2]
        xr_h, xr_l = _split(zr_sc[:, cols1])
        xi_h, xi_l = _split(zi_sc[:, cols1])
        yq = (_dot(u2.astype(BF16), m_ref[q]) + _dot(xr_h, pr_ref[q]) + _dot(xr_l, pr_ref[q])
              + _dot(xi_h, pi_ref[q]) + _dot(xi_l, pi_ref[q]) + d_ref[:, cols2] * u2)
        ya_sc[:, cols2] = _gelu_tanh(yq)

    for blk in range(S5_WIDTH // LANES):
        for t in range(S5_TOK):
            acc = jnp.zeros((rows, LANES), F32)
            for gl in range(groups_per_vreg):
                g = blk * groups_per_vreg + gl
                src = ya_sc[:, g * LANES:(g + 1) * LANES]
                sh = (S5_GROUP * (gl - t)) % LANES
                if sh:
                    src = pltpu.roll(src, sh, 1)
                acc = jnp.where(lane_t == gl, src, acc)
            y_ref[blk, pl.ds(t, rows, stride=S5_TOK), :] = acc


def _s5(u, state_in, tab, rows):
    _, bsz, t_len, _ = u.shape
    tm = rows * S5_TOK
    full = lambda a: pl.BlockSpec(a.shape, lambda b, i: (0,) * a.ndim)
    weights = [tab["m"], tab["qr"], tab["qi"], tab["pr"], tab["pi"], tab["lr"], tab["li"], tab["d"]]
    return pl.pallas_call(
        functools.partial(_s5_kernel, rows=rows),
        grid=(bsz, t_len // tm),
        in_specs=[_lane_blocks(tm), full(state_in)] + [full(a) for a in weights],
        out_specs=[_lane_blocks(tm), pl.BlockSpec((None, 2, S5_NSTATE), lambda b, i: (b, 0, 0))],
        out_shape=[jax.ShapeDtypeStruct(u.shape, F32),
                   jax.ShapeDtypeStruct((bsz, 2, S5_NSTATE), F32)],
        scratch_shapes=[pltpu.VMEM((2, S5_NSTATE), F32), pltpu.VMEM((rows, S5_COLS), F32),
                        pltpu.VMEM((rows, S5_NSTATE), F32), pltpu.VMEM((rows, S5_NSTATE), F32),
                        pltpu.VMEM((rows, S5_COLS), F32)],
        compiler_params=_params("parallel", "arbitrary"),
        name="s5",
    )(u, state_in, *weights)


def _s5_tables(lam_re, lam_im, log_dt, b_re, b_im, c_re, c_im, d, rows):
    f = lambda a: a.astype(F32)
    re = jnp.minimum(f(lam_re), -1e-4)
    im = f(lam_im)
    dt = jnp.exp(f(log_dt))[:, None]

    def power(m):
        mag = jnp.exp(m * (re * dt))
        ang = m * (im * dt)
        return mag * jnp.cos(ang), mag * jnp.sin(ang)

    l1r, l1i = power(1.0)
    den = re * re + im * im
    cr = ((l1r - 1.0) * re + l1i * im) / den
    ci = (l1i * re - (l1r - 1.0) * im) / den
    bbr = cr[..., None] * f(b_re) - ci[..., None] * f(b_im)
    bbi = cr[..., None] * f(b_im) + ci[..., None] * f(b_re)
    pw = [power(float(m)) for m in range(S5_TOK + 1)]
    pwr = jnp.stack([x[0] for x in pw])
    pwi = jnp.stack([x[1] for x in pw])
    ccr, cci = f(c_re), f(c_im)

    pbr = pwr[:S5_TOK, :, :, None] * bbr[None] - pwi[:S5_TOK, :, :, None] * bbi[None]
    pbi = pwr[:S5_TOK, :, :, None] * bbi[None] + pwi[:S5_TOK, :, :, None] * bbr[None]
    kern = (jnp.einsum("gon,mgnh->mgoh", ccr, pbr, precision=HI)
            - jnp.einsum("gon,mgnh->mgoh", cci, pbi, precision=HI))
    ti = jnp.arange(S5_TOK)[:, None]
    to = jnp.arange(S5_TOK)[None, :]
    diff = to - ti
    blocks = jnp.where((diff >= 0)[:, :, None, None, None], kern[jnp.clip(diff, 0, S5_TOK - 1)], 0.0)
    m_g = blocks.transpose(2, 0, 4, 1, 3).reshape(S5_GROUPS, LANES, LANES)

    q_r = pbr[::-1].transpose(1, 0, 3, 2).reshape(S5_GROUPS, LANES, S5_STATE)
    q_i = pbi[::-1].transpose(1, 0, 3, 2).reshape(S5_GROUPS, LANES, S5_STATE)
    cpr = ccr[None] * pwr[1:, :, None, :] - cci[None] * pwi[1:, :, None, :]
    cpi = ccr[None] * pwi[1:, :, None, :] + cci[None] * pwr[1:, :, None, :]
    p_r = cpr.transpose(1, 3, 0, 2).reshape(S5_GROUPS, S5_STATE, LANES)
    p_i = (-cpi).transpose(1, 3, 0, 2).reshape(S5_GROUPS, S5_STATE, LANES)

    def pair_diag(a):
        n_in, n_out = a.shape[1:]
        a = a.reshape(S5_PAIRS, 2, n_in, n_out)
        z = jnp.zeros((S5_PAIRS, n_in, n_out), a.dtype)
        top = jnp.concatenate([a[:, 0], z], axis=2)
        bot = jnp.concatenate([z, a[:, 1]], axis=2)
        return jnp.concatenate([top, bot], axis=1).astype(BF16)

    levels = max(1, int(math.log2(rows)))
    lv = [power(float(S5_TOK * (2 ** kk))) for kk in range(levels)]
    lr = jnp.stack([x[0].reshape(-1) for x in lv])
    li = jnp.stack([x[1].reshape(-1) for x in lv])
    d_cols = jnp.broadcast_to(f(d).reshape(S5_GROUPS, 1, S5_GROUP), (S5_GROUPS, S5_TOK, S5_GROUP)).reshape(1, -1)
    return {"m": pair_diag(m_g), "qr": pair_diag(q_r), "qi": pair_diag(q_i), "pr": pair_diag(p_r),
            "pi": pair_diag(p_i), "lr": lr, "li": li, "d": d_cols}


def _mix_kernel(h_ref, gate_ref, ya_ref, yb_ref, wbr_ref, wglu_ref, wout_ref, gpost_ref, gpre_ref,
                h1_ref, hn_ref):
    y_a = _dot(ya_ref[...], wbr_ref[...])
    yb = jnp.concatenate([yb_ref[blk] for blk in range(S5_WIDTH // LANES)], axis=1)
    z = _dot(yb.astype(BF16), wglu_ref[...])
    y_b = z[:, :D_MODEL] * jax.nn.sigmoid(z[:, D_MODEL:])
    gates = gate_ref[...].astype(F32)
    mix = (gates[:, :D_MODEL] * y_a + gates[:, D_MODEL:] * y_b).astype(BF16)
    h1 = h_ref[...] + _rms(_dot(mix, wout_ref[...]), gpost_ref[...])
    h1_ref[...] = h1
    hn_ref[...] = _rms(h1, gpre_ref[...]).astype(BF16)


def _mix(h, gates, ya, yb, prm, tm):
    bsz, t_len, _ = h.shape
    row = lambda c: pl.BlockSpec((None, tm, c), lambda b, i: (b, i, 0))
    full = lambda a: pl.BlockSpec(a.shape, lambda b, i: (0,) * a.ndim)
    weights = [prm["w_branch"], prm["w_glu"], prm["w_out"], prm["g_post"], prm["g_ffn_pre"]]
    return pl.pallas_call(
        _mix_kernel,
        grid=(bsz, t_len // tm),
        in_specs=[row(D_MODEL), row(2 * D_MODEL), row(RWKV_WIDTH), _lane_blocks(tm)] + [full(a) for a in weights],
        out_specs=[row(D_MODEL), row(D_MODEL)],
        out_shape=[jax.ShapeDtypeStruct((bsz, t_len, D_MODEL), F32),
                   jax.ShapeDtypeStruct((bsz, t_len, D_MODEL), BF16)],
        compiler_params=_params("parallel", "parallel"),
        name="mix",
    )(h, gates, ya, yb, *weights)


def _ffn_kernel(hn_ref, h1_ref, halo_ref, wa_ref, wb_ref, cw_ref, cb_ref, wd_ref, gpost_ref,
                out_ref, halo_out_ref, halo_sc, acc_sc, *, tm):
    i = pl.program_id(1)
    f = pl.program_id(2)
    n_f = pl.num_programs(2)

    @pl.when(i == 0)
    def _():
        halo_sc[f] = halo_ref[...]

    hn = hn_ref[...]
    ua = _dot(hn, wa_ref[...])
    ub = _dot(hn, wb_ref[...])
    row = lax.broadcasted_iota(jnp.int32, ua.shape, 0)
    halo = halo_sc[f]
    prev1 = halo[7:8, :]
    prev2 = halo[6:7, :]
    u1 = jnp.where(row == 0, prev1, pltpu.roll(ua, 1, 0))
    u2 = jnp.where(row == 0, prev2, jnp.where(row == 1, prev1, pltpu.roll(ua, 2, 0)))
    conv = cw_ref[0:1, :] * u2 + cw_ref[1:2, :] * u1 + cw_ref[2:3, :] * ua + cb_ref[...]
    tail = ua[tm - 8:, :]
    halo_sc[f] = tail
    halo_out_ref[...] = tail
    act = (_gelu_tanh(conv) * ub).astype(BF16)
    part = _dot(act, wd_ref[...])

    @pl.when(f == 0)
    def _():
        acc_sc[...] = part

    @pl.when(f > 0)
    def _():
        acc_sc[...] += part

    @pl.when(f == n_f - 1)
    def _():
        out_ref[...] = h1_ref[...] + _rms(acc_sc[...], gpost_ref[...])


def _ffn(hn, h1, halo_in, prm, tm, tf):
    bsz, t_len, _ = hn.shape
    n_f = D_FF // tf
    n_t = t_len // tm
    row = lambda c: pl.BlockSpec((None, tm, c), lambda b, i, f: (b, i, 0))
    return pl.pallas_call(
        functools.partial(_ffn_kernel, tm=tm),
        grid=(bsz, n_t, n_f),
        in_specs=[row(D_MODEL), row(D_MODEL),
                  pl.BlockSpec((8, tf), lambda b, i, f: (0, f)),
                  pl.BlockSpec((D_MODEL, tf), lambda b, i, f: (0, f)),
                  pl.BlockSpec((D_MODEL, tf), lambda b, i, f: (0, n_f + f)),
                  pl.BlockSpec((3, tf), lambda b, i, f: (0, f)),
                  pl.BlockSpec((1, tf), lambda b, i, f: (0, f)),
                  pl.BlockSpec((tf, D_MODEL), lambda b, i, f: (f, 0)),
                  pl.BlockSpec((1, D_MODEL), lambda b, i, f: (0, 0))],
        out_specs=[row(D_MODEL), pl.BlockSpec((None, None, 8, tf), lambda b, i, f: (b, i, 0, f))],
        out_shape=[jax.ShapeDtypeStruct((bsz, t_len, D_MODEL), F32),
                   jax.ShapeDtypeStruct((bsz, n_t, 8, D_FF), F32)],
        scratch_shapes=[pltpu.VMEM((n_f, 8, tf), F32), pltpu.VMEM((tm, D_MODEL), F32)],
        compiler_params=_params("parallel", "arbitrary", "arbitrary"),
        name="ffn",
    )(hn, h1, halo_in, prm["w_up"], prm["w_up"], prm["conv_w"], prm["conv_b"], prm["w_down"], prm["g_ffn_post"])


def _pick(t_len, options):
    for o in options:
        if t_len % o == 0:
            return o
    raise ValueError(f"no tile for sequence length {t_len}")


def _block(h, carry, prm, tab_fn, n_pad=0):
    t_len = h.shape[1]
    keep = (jnp.arange(t_len) >= n_pad)[None, :, None]
    tm_proj = _pick(t_len, (256, 64))
    tm_rwkv = _pick(t_len, (256, 64))
    rows_s5 = _pick(t_len // S5_TOK, (128, 8))
    tm_mix = _pick(t_len, (512, 64))
    tm_ffn = _pick(t_len, (512, 64))
    p, u, gates = _inproj(h, prm["g_pre"], prm["w_in"], tm_proj)
    ya, rwkv_state = _rwkv(p, carry["prev_row"], carry["rwkv_state"], prm, tm_rwkv)
    yb, s5_state = _s5(u, carry["s5_state"], tab_fn(rows_s5), rows_s5)
    h1, hn2 = _mix(h, gates, ya, yb, prm, tm_mix)
    if n_pad:
        hn2 = jnp.where(keep, hn2, jnp.zeros_like(hn2))
    out, halo = _ffn(hn2, h1, carry["halo"], prm, tm_ffn, 1408)
    if n_pad:
        out = jnp.where(keep, out, jnp.zeros_like(out))
    new_carry = {"prev_row": p[0, -1:, :], "rwkv_state": rwkv_state[0], "s5_state": s5_state[0],
                 "halo": halo[0, -1]}
    return out, new_carry


def kernel(x, meta_tokens, norm_mix_pre, norm_mix_post, w_in, rwkv_shift_mu, rwkv_w0, rwkv_w_up, rwkv_a0, rwkv_a_up, rwkv_g_up, rwkv_k_k, rwkv_k_a, rwkv_r_k, rwkv_ln_gain, rwkv_ln_bias, rwkv_w_branch, s5_lambda_re, s5_lambda_im, s5_log_dt, s5_b_re, s5_b_im, s5_c_re, s5_c_im, s5_d, s5_w_glu, w_out, norm_ffn_pre, norm_ffn_post, ffn_w_up, ffn_conv_w, ffn_conv_b, ffn_w_down):
    depth = w_in.shape[0]
    n_pairs = RWKV_WIDTH // LANES
    zeros_lora = jnp.zeros((W_LORA, RWKV_WIDTH), F32)

    h_meta = jnp.concatenate([jnp.zeros((META_PAD - N_META, D_MODEL), x.dtype), meta_tokens.astype(x.dtype)])[None]
    h_main = x
    for l in range(depth):
        vec = lambda a: a[l].reshape(1, -1).astype(F32)
        prm = {
            "g_pre": vec(norm_mix_pre), "g_post": vec(norm_mix_post),
            "g_ffn_pre": vec(norm_ffn_pre), "g_ffn_post": vec(norm_ffn_post),
            "w_in": w_in[l].astype(BF16),
            "mu": vec(rwkv_shift_mu), "w0": vec(rwkv_w0), "a0": vec(rwkv_a0),
            "wup": jnp.concatenate([rwkv_w_up[l].astype(F32), zeros_lora]),
            "aup": jnp.concatenate([zeros_lora, rwkv_a_up[l].astype(F32)]),
            "gup": rwkv_g_up[l].astype(F32),
            "k_k": vec(rwkv_k_k), "k_a": vec(rwkv_k_a), "r_k": vec(rwkv_r_k),
            "ln_g": vec(rwkv_ln_gain), "ln_b": vec(rwkv_ln_bias),
            "w_branch": rwkv_w_branch[l].astype(BF16), "w_glu": s5_w_glu[l].astype(BF16),
            "w_out": w_out[l].astype(BF16), "w_up": ffn_w_up[l].astype(BF16),
            "conv_w": ffn_conv_w[l].astype(F32), "conv_b": vec(ffn_conv_b),
            "w_down": ffn_w_down[l].astype(BF16),
        }
        tab_fn = functools.partial(_s5_tables, s5_lambda_re[l], s5_lambda_im[l], s5_log_dt[l], s5_b_re[l],
                                   s5_b_im[l], s5_c_re[l], s5_c_im[l], s5_d[l])
        zero_carry = {"prev_row": jnp.zeros((1, RWKV_PROJ), F32),
                      "rwkv_state": jnp.zeros((n_pairs, LANES, LANES), F32),
                      "s5_state": jnp.zeros((2, S5_NSTATE), F32),
                      "halo": jnp.zeros((8, D_FF), F32)}
        h_meta, carry = _block(h_meta, zero_carry, prm, tab_fn, n_pad=META_PAD - N_META)
        h_main, _ = _block(h_main, carry, prm, tab_fn)
    return h_main
```

```python
import functools
import math

import jax
import jax.numpy as jnp
from jax import lax
from jax.experimental import pallas as pl
from jax.experimental.pallas import tpu as pltpu

F32 = jnp.float32
BF16 = jnp.bfloat16
HI = lax.Precision.HIGHEST

D_MODEL = 1024
N_META = 16
RWKV_WIDTH = 512
HEAD_DIM = 64
W_LORA = 64
A_LORA = 64
G_LORA = 128
RWKV_PROJ = 3 * RWKV_WIDTH + W_LORA + A_LORA + G_LORA
GN_EPS = 64e-5
S5_WIDTH = 512
S5_GROUP = 16
S5_STATE = 64
D_FF = 2816
RMS_EPS = 1e-6

LANES = 128
CHUNK = 64
CHUNK_GROUP = 4
S5_TOK = 8
S5_BLOCK_STATE = (LANES // S5_GROUP) * S5_STATE
META_PAD = 64
VMEM_LIMIT = 56 * 1024 * 1024


def _dot(a, b, precision=None):
    return jnp.dot(a, b, preferred_element_type=F32, precision=precision)


def _dot_nt(a, b):
    return lax.dot_general(a, b, (((1,), (1,)), ((), ())), preferred_element_type=F32)


def _dot_tn(a, b):
    return lax.dot_general(a, b, (((0,), (0,)), ((), ())), preferred_element_type=F32)


def _bf(x):
    return x.astype(BF16)


def _split(x):
    hi = x.astype(BF16)
    return hi, (x - hi.astype(F32)).astype(BF16)


def _dot_x3(a, b):
    ah, al = _split(a)
    bh, bl = _split(b)
    return _dot(ah, bh) + _dot(al, bh) + _dot(ah, bl)


def _gelu_tanh(x):
    c = math.sqrt(2.0 / math.pi)
    return 0.5 * x * (1.0 + jnp.tanh(c * (x + 0.044715 * (x * x * x))))


def _softplus(x):
    return jnp.maximum(x, 0.0) + jnp.log(1.0 + jnp.exp(-jnp.abs(x)))


def _rms(x, g):
    ms = jnp.mean(x * x, axis=-1, keepdims=True)
    return x * lax.rsqrt(ms + RMS_EPS) * g


def _params(*sem):
    return pltpu.CompilerParams(dimension_semantics=sem, vmem_limit_bytes=VMEM_LIMIT)


def _resident(a):
    return pl.BlockSpec(a.shape, lambda b, i: (0,) * a.ndim, pipeline_mode=pl.Buffered(1))


def _lane_blocks(tm):
    return pl.BlockSpec((S5_WIDTH // LANES, None, tm, LANES), lambda b, i: (0, b, i, 0))


def _inproj_kernel(h_ref, g_ref, w_ref, p_ref, u_ref, gate_ref):
    hn = _rms(h_ref[...], g_ref[...]).astype(BF16)
    p_ref[...] = _dot(hn, w_ref[:, 0:RWKV_PROJ])
    u = _dot(hn, w_ref[:, RWKV_PROJ:RWKV_PROJ + S5_WIDTH])
    for blk in range(S5_WIDTH // LANES):
        u_ref[blk] = u[:, blk * LANES:(blk + 1) * LANES]
    gate_ref[...] = jax.nn.sigmoid(_dot(hn, w_ref[:, RWKV_PROJ + S5_WIDTH:])).astype(BF16)


def _inproj(h, g, w, tm):
    bsz, t_len, _ = h.shape
    n_w = w.shape[1]
    row = lambda c: pl.BlockSpec((None, tm, c), lambda b, i: (b, i, 0))
    full = lambda a: pl.BlockSpec(a.shape, lambda b, i: (0,) * a.ndim)
    return pl.pallas_call(
        _inproj_kernel,
        grid=(bsz, t_len // tm),
        in_specs=[row(D_MODEL), full(g), full(w)],
        out_specs=[row(RWKV_PROJ), _lane_blocks(tm), row(n_w - RWKV_PROJ - S5_WIDTH)],
        out_shape=[jax.ShapeDtypeStruct((bsz, t_len, RWKV_PROJ), F32),
                   jax.ShapeDtypeStruct((S5_WIDTH // LANES, bsz, t_len, LANES), F32),
                   jax.ShapeDtypeStruct((bsz, t_len, n_w - RWKV_PROJ - S5_WIDTH), BF16)],
        compiler_params=_params("parallel", "parallel"),
        name="inproj",
    )(h, g, w)


def _rwkv_kernel(p_ref, prev_ref, sin_ref, mu_ref, w0_ref, wup_ref, a0_ref, aup_ref, gup_ref,
                 kk_ref, ka_ref, rk_ref, lng_ref, lnb_ref,
                 y_ref, sout_ref,
                 prev_sc, state_sc, at_sc, rt_sc, bh_sc, kh_sc, bl_sc, kl_sc, v_sc, gl_sc,
                 bon_sc, g_sc, rp_sc, yl_sc, phi_sc, psi_sc, *, tm):
    L = CHUNK
    n_chunks = tm // L
    n_pairs = RWKV_WIDTH // LANES
    n2 = 2 * L

    @pl.when(pl.program_id(1) == 0)
    def _():
        prev_sc[...] = prev_ref[...]
        state_sc[...] = sin_ref[...]

    si = lax.broadcasted_iota(jnp.int32, (n2, n2), 0)
    sj = lax.broadcasted_iota(jnp.int32, (n2, n2), 1)
    same_head = (si // L) == (sj // L)
    strict = same_head & ((si % L) > (sj % L))
    incl = same_head & ((si % L) >= (sj % L))
    eye_mask = si == sj
    eye = eye_mask.astype(F32)
    head_sum = same_head.astype(BF16)
    head0 = lax.broadcasted_iota(jnp.int32, (L, LANES), 1) < HEAD_DIM

    p = p_ref[...]
    row = lax.broadcasted_iota(jnp.int32, p.shape, 0)
    ps = jnp.where(row == 0, prev_sc[...], pltpu.roll(p, 1, 0))
    prev_sc[...] = p_ref[pl.ds(tm - 1, 1), :]
    pm = p + (ps - p) * mu_ref[...]
    r = pm[:, 0:RWKV_WIDTH]
    k = pm[:, RWKV_WIDTH:2 * RWKV_WIDTH]
    v = pm[:, 2 * RWKV_WIDTH:3 * RWKV_WIDTH]
    wa = pm[:, 3 * RWKV_WIDTH:3 * RWKV_WIDTH + W_LORA + A_LORA]
    gd = pm[:, 3 * RWKV_WIDTH + W_LORA + A_LORA:]

    w_log = -_softplus(-(w0_ref[...] + _dot_x3(jnp.tanh(wa), wup_ref[...]))) - 0.5
    logw = -jnp.exp(w_log)
    a = jax.nn.sigmoid(a0_ref[...] + _dot_x3(wa, aup_ref[...]))
    g_sc[...] = _dot(_bf(jax.nn.sigmoid(gd)), _bf(gup_ref[...]))
    kx = k * kk_ref[...]
    kx2 = kx * kx
    rk = r * k
    v_sc[...] = v

    li = lax.broadcasted_iota(jnp.int32, (L, L), 0)
    lj = lax.broadcasted_iota(jnp.int32, (L, L), 1)
    ltri = (li >= lj).astype(BF16)
    for j in range(n_pairs):
        cols = slice(j * LANES, (j + 1) * LANES)
        a_j = a[:, cols]
        kkn = kx[:, cols] / jnp.maximum(jnp.sqrt(_dot(_bf(kx2[:, cols]), head_sum)), 1e-12)
        scale = 1.0 + (a_j - 1.0) * ka_ref[:, cols]
        kp = k[:, cols] * scale
        b = kkn * a_j
        bon_sc[:, cols] = _dot(_bf(rk[:, cols] * scale * rk_ref[:, cols]), head_sum) * v[:, cols]
        for c in range(n_chunks):
            rows = slice(c * L, (c + 1) * L)
            lw = logw[rows, cols]
            lw_hi, lw_lo = _split(lw)
            cum = _dot(ltri, lw_hi) + _dot(ltri, lw_lo)
            cum_l = jnp.sum(lw, axis=0, keepdims=True)
            at_sc[rows, cols] = -kkn[rows] * jnp.exp(cum - lw)
            rt_sc[rows, cols] = r[rows, cols] * jnp.exp(cum)
            ginv = jnp.exp(-cum)
            bh_sc[rows, cols] = b[rows] * ginv
            kh_sc[rows, cols] = kp[rows] * ginv
            tail = jnp.exp(cum_l - cum)
            bl_sc[rows, cols] = b[rows] * tail
            kl_sc[rows, cols] = kp[rows] * tail
            gl_sc[c * 8:(c + 1) * 8, cols] = jnp.broadcast_to(jnp.exp(cum_l), (8, LANES))

    def split_heads(x):
        return jnp.concatenate([jnp.where(head0, x, 0.0), jnp.where(head0, 0.0, x)], axis=0)

    def pick_heads(xs):
        return jnp.where(head0, xs[:L], xs[L:])

    def merge_heads(xs):
        return xs[:L] + xs[L:]

    group = min(CHUNK_GROUP, n_chunks)

    def local_body(cg, carry):
        units = [(cg * group + dc, j) for dc in range(group) for j in range(n_pairs)]
        n_u = range(len(units))
        rws = [pl.ds(pl.multiple_of(c * L, L), L) for c, _ in units]
        cl = [slice(j * LANES, (j + 1) * LANES) for _, j in units]
        at_s = [split_heads(at_sc[rws[i], cl[i]]) for i in n_u]
        rt_s = [split_heads(rt_sc[rws[i], cl[i]]) for i in n_u]
        bl = [_bf(bl_sc[rws[i], cl[i]]) for i in n_u]
        kl = [_bf(kl_sc[rws[i], cl[i]]) for i in n_u]
        vv = [v_sc[rws[i], cl[i]] for i in n_u]
        gl = [gl_sc[pl.ds(pl.multiple_of(units[i][0] * 8, 8), 1), cl[i]] for i in n_u]
        bk_s = [_bf(jnp.concatenate([split_heads(bh_sc[rws[i], cl[i]]), split_heads(kh_sc[rws[i], cl[i]])],
                                    axis=0)) for i in n_u]
        at_b = [_bf(x) for x in at_s]
        lhs = [jnp.concatenate([at_b[i], _bf(rt_s[i])], axis=0) for i in n_u]
        abk = [_dot_nt(lhs[i], bk_s[i]) for i in n_u]
        a_ab = [jnp.where(strict, abk[i][:n2, :n2], 0.0) for i in n_u]
        a_ak = [_bf(jnp.where(strict, abk[i][:n2, n2:], 0.0)) for i in n_u]
        a_rb = [_bf(jnp.where(incl, abk[i][n2:, :n2], 0.0)) for i in n_u]
        a_rk = [_bf(jnp.where(incl, abk[i][n2:, n2:], 0.0)) for i in n_u]
        first = ((si // 2) == (sj // 2)) & ((si % 2) == 1) & ((sj % 2) == 0)
        tinv = [eye + jnp.where(first, a_ab[i], 0.0) for i in n_u]
        s = 2
        while s < L:
            off = ((si // (2 * s)) == (sj // (2 * s))) & ((si % (2 * s)) >= s) & ((sj % (2 * s)) < s)
            t_b = [_bf(t) for t in tinv]
            x1 = [_dot(_bf(jnp.where(off, a_ab[i], 0.0)), t_b[i]) for i in n_u]
            x2 = [_dot(t_b[i], _bf(x1[i])) for i in n_u]
            tinv = [tinv[i] + x2[i] for i in n_u]
            s *= 2
        v2 = [_bf(jnp.concatenate([vv[i], vv[i]], axis=0)) for i in n_u]
        av = [_dot(a_ak[i], v2[i]) for i in n_u]
        tx = [_dot(_bf(tinv[i]), jnp.concatenate([at_b[i], _bf(av[i])], axis=1)) for i in n_u]
        g1 = [_dot(a_rb[i], _bf(tx[i])) for i in n_u]
        g2 = [_dot(a_rk[i], v2[i]) for i in n_u]
        w = [_bf(merge_heads(tx[i][:, :LANES])) for i in n_u]
        uv = [_bf(jnp.concatenate([pick_heads(tx[i][:, LANES:]), vv[i]], axis=0)) for i in n_u]
        phi = [_dot_tn(w[i], bl[i]) for i in n_u]
        psi = [_dot_tn(uv[i], jnp.concatenate([bl[i], kl[i]], axis=0)) for i in n_u]
        for i in n_u:
            c, j = units[i]
            rp_sc[rws[i], cl[i]] = merge_heads(rt_s[i] + g1[i][:, :LANES])
            yl_sc[rws[i], cl[i]] = pick_heads(g1[i][:, LANES:] + g2[i])
            phi_sc[c, j] = jnp.where(same_head, phi[i], 0.0) + jnp.where(eye_mask, gl[i], 0.0)
            psi_sc[c, j] = jnp.where(same_head, psi[i], 0.0)
        return carry

    lax.fori_loop(0, n_chunks // group, local_body, 0)

    def state_body(c, carry):
        rws = pl.ds(pl.multiple_of(c * L, L), L)
        pairs = range(n_pairs)
        cl = [slice(j * LANES, (j + 1) * LANES) for j in pairs]
        st = [_split(state_sc[j]) for j in pairs]
        ph = [_split(phi_sc[c, j]) for j in pairs]
        rp = [_bf(rp_sc[rws, cl[j]]) for j in pairs]
        y0 = [_dot_nt(rp[j], st[j][0]) for j in pairs]
        n0 = [_dot(st[j][0], ph[j][0]) for j in pairs]
        n1 = [_dot(st[j][1], ph[j][0]) for j in pairs]
        n2_ = [_dot(st[j][0], ph[j][1]) for j in pairs]
        for j in pairs:
            rp_sc[rws, cl[j]] = y0[j] + yl_sc[rws, cl[j]]
            state_sc[j] = n0[j] + n1[j] + n2_[j] + psi_sc[c, j]
        return carry

    lax.fori_loop(0, n_chunks, state_body, 0)

    inv_n = 1.0 / HEAD_DIM
    for j in range(n_pairs):
        cols = slice(j * LANES, (j + 1) * LANES)
        y = rp_sc[:, cols]
        yc = y - _dot(_bf(y), head_sum) * inv_n
        var = _dot(_bf(yc * yc), head_sum) * inv_n
        yn = yc * lax.rsqrt(var + GN_EPS) * lng_ref[:, cols] + lnb_ref[:, cols]
        y_ref[:, cols] = ((yn + bon_sc[:, cols]) * g_sc[:, cols]).astype(BF16)
    sout_ref[...] = state_sc[...]


def _rwkv(p, prev_row, state_in, prm, tm):
    bsz, t_len, _ = p.shape
    n_pairs = RWKV_WIDTH // LANES
    n_chunks = tm // CHUNK
    full = lambda a: pl.BlockSpec(a.shape, lambda b, i: (0,) * a.ndim)
    weights = [prm["mu"], prm["w0"], prm["wup"], prm["a0"], prm["aup"], prm["gup"], prm["k_k"], prm["k_a"],
               prm["r_k"], prm["ln_g"], prm["ln_b"]]
    tile = pltpu.VMEM((tm, RWKV_WIDTH), F32)
    mats = pltpu.VMEM((n_chunks, n_pairs, LANES, LANES), F32)
    return pl.pallas_call(
        functools.partial(_rwkv_kernel, tm=tm),
        grid=(bsz, t_len // tm),
        in_specs=[pl.BlockSpec((None, tm, RWKV_PROJ), lambda b, i: (b, i, 0)), full(prev_row), full(state_in)]
                 + [full(a) for a in weights],
        out_specs=[pl.BlockSpec((None, tm, RWKV_WIDTH), lambda b, i: (b, i, 0)),
                   pl.BlockSpec((None, n_pairs, LANES, LANES), lambda b, i: (b, 0, 0, 0))],
        out_shape=[jax.ShapeDtypeStruct((bsz, t_len, RWKV_WIDTH), BF16),
                   jax.ShapeDtypeStruct((bsz, n_pairs, LANES, LANES), F32)],
        scratch_shapes=[pltpu.VMEM((1, RWKV_PROJ), F32), pltpu.VMEM((n_pairs, LANES, LANES), F32)]
                       + [tile] * 7 + [pltpu.VMEM((n_chunks * 8, RWKV_WIDTH), F32)] + [tile] * 4 + [mats] * 2,
        compiler_params=_params("parallel", "arbitrary"),
        name="rwkv",
    )(p, prev_row, state_in, *weights)


def _s5_kernel(u_ref, sin_ref, m_ref, q_ref, p_ref, lr_ref, li_ref, d_ref,
               y_ref, sout_ref, st_sc, *, rows):
    half = S5_BLOCK_STATE

    @pl.when(pl.program_id(2) == 0)
    def _():
        st_sc[...] = sin_ref[...]

    u = jnp.concatenate([u_ref[pl.ds(t, rows, stride=S5_TOK), :] for t in range(S5_TOK)], axis=1)
    ub = u.astype(BF16)
    z = _dot(ub, q_ref[...])
    y_in = _dot(ub, m_ref[...]) + d_ref[...] * u

    row = lax.broadcasted_iota(jnp.int32, (rows, half), 0)
    sub = row % 8
    xr = z[:, :half]
    xi = z[:, half:]
    s = 1
    while s < min(8, rows):
        ar = lr_ref[s - 1:s, :]
        ai = li_ref[s - 1:s, :]
        pr = jnp.where(sub >= s, pltpu.roll(xr, s, 0), 0.0)
        pi = jnp.where(sub >= s, pltpu.roll(xi, s, 0), 0.0)
        xr, xi = xr + (ar * pr - ai * pi), xi + (ar * pi + ai * pr)
        s *= 2
    tr = lr_ref[...]
    ti = li_ref[...]
    cr = st_sc[0:1, :]
    ci = st_sc[1:2, :]
    out_r, out_i = [], []
    for blk in range(rows // 8):
        br = xr[blk * 8:(blk + 1) * 8] + (tr * cr - ti * ci)
        bi = xi[blk * 8:(blk + 1) * 8] + (tr * ci + ti * cr)
        out_r.append(br)
        out_i.append(bi)
        cr = br[7:8]
        ci = bi[7:8]
    xr = jnp.concatenate(out_r, axis=0)
    xi = jnp.concatenate(out_i, axis=0)
    sr = st_sc[0:1, :]
    si = st_sc[1:2, :]
    st_sc[0:1, :] = cr
    st_sc[1:2, :] = ci
    sout_ref[...] = st_sc[...]
    x0 = jnp.concatenate([jnp.where(row == 0, sr, pltpu.roll(xr, 1, 0)),
                          jnp.where(row == 0, si, pltpu.roll(xi, 1, 0))], axis=1)
    ya = _gelu_tanh(y_in + _dot(_bf(x0), p_ref[...]))
    for t in range(S5_TOK):
        y_ref[pl.ds(t, rows, stride=S5_TOK), :] = ya[:, t * LANES:(t + 1) * LANES]


def _s5(u, state_in, tab, rows):
    n_blk, bsz, t_len, _ = u.shape
    tm = rows * S5_TOK
    blk = lambda a: pl.BlockSpec((None,) + a.shape[1:], lambda b, k, i: (k,) + (0,) * (a.ndim - 1))
    tokens = pl.BlockSpec((None, None, tm, LANES), lambda b, k, i: (k, b, i, 0))
    weights = [tab["m"], tab["q"], tab["p"], tab["lr"], tab["li"], tab["d"]]
    return pl.pallas_call(
        functools.partial(_s5_kernel, rows=rows),
        grid=(bsz, n_blk, t_len // tm),
        in_specs=[tokens, blk(state_in)] + [blk(a) for a in weights],
        out_specs=[tokens, pl.BlockSpec((None, None, 2, S5_BLOCK_STATE), lambda b, k, i: (b, k, 0, 0))],
        out_shape=[jax.ShapeDtypeStruct(u.shape, F32),
                   jax.ShapeDtypeStruct((bsz, n_blk, 2, S5_BLOCK_STATE), F32)],
        scratch_shapes=[pltpu.VMEM((2, S5_BLOCK_STATE), F32)],
        compiler_params=_params("parallel", "arbitrary", "arbitrary"),
        name="s5",
    )(u, state_in, *weights)


def _s5_tables(lam_re, lam_im, log_dt, b_re, b_im, c_re, c_im, d, rows):
    f = lambda a: a.astype(F32)
    re = jnp.minimum(f(lam_re), -1e-4)
    im = f(lam_im)
    dt = jnp.exp(f(log_dt))[:, None]

    def power(m):
        mag = jnp.exp(m * (re * dt))
        ang = m * (im * dt)
        return mag * jnp.cos(ang), mag * jnp.sin(ang)

    l1r, l1i = power(1.0)
    den = re * re + im * im
    cr = ((l1r - 1.0) * re + l1i * im) / den
    ci = (l1i * re - (l1r - 1.0) * im) / den
    bbr = cr[..., None] * f(b_re) - ci[..., None] * f(b_im)
    bbi = cr[..., None] * f(b_im) + ci[..., None] * f(b_re)
    pw = [power(float(m)) for m in range(S5_TOK + 1)]
    pwr = jnp.stack([x[0] for x in pw])
    pwi = jnp.stack([x[1] for x in pw])
    ccr, cci = f(c_re), f(c_im)
    n_blk = S5_WIDTH // LANES
    gpb = LANES // S5_GROUP
    eye = jnp.eye(gpb, dtype=F32)

    pbr = pwr[:S5_TOK, :, :, None] * bbr[None] - pwi[:S5_TOK, :, :, None] * bbi[None]
    pbi = pwr[:S5_TOK, :, :, None] * bbi[None] + pwi[:S5_TOK, :, :, None] * bbr[None]
    kern = (jnp.einsum("gon,mgnh->mgoh", ccr, pbr, precision=HI)
            - jnp.einsum("gon,mgnh->mgoh", cci, pbi, precision=HI))
    ti = jnp.arange(S5_TOK)[:, None]
    to = jnp.arange(S5_TOK)[None, :]
    diff = to - ti
    toep = jnp.where((diff >= 0)[:, :, None, None, None], kern[jnp.clip(diff, 0, S5_TOK - 1)], 0.0)
    m6 = toep.reshape(S5_TOK, S5_TOK, n_blk, gpb, S5_GROUP, S5_GROUP).transpose(2, 0, 3, 5, 1, 4)
    m_big = (m6[:, :, :, :, :, None, :] * eye[None, None, :, None, None, :, None]).reshape(n_blk, 8 * LANES, 8 * LANES)

    q2 = jnp.stack([pbr[::-1], pbi[::-1]])
    q6 = q2.reshape(2, S5_TOK, n_blk, gpb, S5_STATE, S5_GROUP).transpose(2, 1, 3, 5, 0, 4)
    q_big = (q6[:, :, :, :, :, None, :] * eye[None, None, :, None, None, :, None]).reshape(
        n_blk, 8 * LANES, 2 * gpb * S5_STATE)
    cpr = ccr[None] * pwr[1:, :, None, :] - cci[None] * pwi[1:, :, None, :]
    cpi = ccr[None] * pwi[1:, :, None, :] + cci[None] * pwr[1:, :, None, :]
    p2 = jnp.stack([cpr, -cpi])
    p6 = p2.reshape(2, S5_TOK, n_blk, gpb, S5_GROUP, S5_STATE).transpose(2, 0, 3, 5, 1, 4)
    p_big = (p6[:, :, :, :, :, None, :] * eye[None, None, :, None, None, :, None]).reshape(
        n_blk, 2 * gpb * S5_STATE, 8 * LANES)

    lv = [power(float(S5_TOK * (rr + 1))) for rr in range(8)]
    lr = jnp.stack([x[0].reshape(n_blk, -1) for x in lv], axis=1)
    li = jnp.stack([x[1].reshape(n_blk, -1) for x in lv], axis=1)
    d_cols = jnp.broadcast_to(f(d).reshape(n_blk, 1, 1, LANES), (n_blk, 1, S5_TOK, LANES)).reshape(n_blk, 1, -1)
    return {"m": m_big.astype(BF16), "q": q_big.astype(BF16), "p": p_big.astype(BF16), "lr": lr, "li": li,
            "d": d_cols}


def _mix_kernel(h_ref, gate_ref, ya_ref, yb_ref, wbr_ref, wglu_ref, wout_ref, gpost_ref, gpre_ref,
                h1_ref, hn_ref):
    y_a = _dot(ya_ref[...], wbr_ref[...])
    yb = jnp.concatenate([yb_ref[blk] for blk in range(S5_WIDTH // LANES)], axis=1)
    z = _dot(yb.astype(BF16), wglu_ref[...])
    y_b = z[:, :D_MODEL] * jax.nn.sigmoid(z[:, D_MODEL:])
    gates = gate_ref[...].astype(F32)
    mix = (gates[:, :D_MODEL] * y_a + gates[:, D_MODEL:] * y_b).astype(BF16)
    h1 = h_ref[...] + _rms(_dot(mix, wout_ref[...]), gpost_ref[...])
    h1_ref[...] = h1
    hn_ref[...] = _rms(h1, gpre_ref[...]).astype(BF16)


def _mix(h, gates, ya, yb, prm, tm):
    bsz, t_len, _ = h.shape
    row = lambda c: pl.BlockSpec((None, tm, c), lambda b, i: (b, i, 0))
    full = lambda a: pl.BlockSpec(a.shape, lambda b, i: (0,) * a.ndim)
    weights = [prm["w_branch"], prm["w_glu"], prm["w_out"], prm["g_post"], prm["g_ffn_pre"]]
    return pl.pallas_call(
        _mix_kernel,
        grid=(bsz, t_len // tm),
        in_specs=[row(D_MODEL), row(2 * D_MODEL), row(RWKV_WIDTH), _lane_blocks(tm)] + [full(a) for a in weights],
        out_specs=[row(D_MODEL), row(D_MODEL)],
        out_shape=[jax.ShapeDtypeStruct((bsz, t_len, D_MODEL), F32),
                   jax.ShapeDtypeStruct((bsz, t_len, D_MODEL), BF16)],
        compiler_params=_params("parallel", "parallel"),
        name="mix",
    )(h, gates, ya, yb, *weights)


def _ffn_kernel(hn_ref, h1_ref, halo_ref, wup_ref, cw_ref, cb_ref, wd_ref, gpost_ref,
                out_ref, halo_out_ref, halo_sc, *, tm, tf):
    @pl.when(pl.program_id(1) == 0)
    def _():
        halo_sc[...] = halo_ref[...]

    hn = hn_ref[...]
    row = lax.broadcasted_iota(jnp.int32, (tm, tf), 0)
    acc = None
    for f in range(D_FF // tf):
        cols = slice(f * tf, (f + 1) * tf)
        ua = _dot(hn, wup_ref[:, cols])
        ub = _dot(hn, wup_ref[:, D_FF + f * tf:D_FF + (f + 1) * tf])
        prev1 = halo_sc[7:8, cols]
        prev2 = halo_sc[6:7, cols]
        u1 = jnp.where(row == 0, prev1, pltpu.roll(ua, 1, 0))
        u2 = jnp.where(row == 0, prev2, jnp.where(row == 1, prev1, pltpu.roll(ua, 2, 0)))
        conv = cw_ref[0:1, cols] * u2 + cw_ref[1:2, cols] * u1 + cw_ref[2:3, cols] * ua + cb_ref[:, cols]
        halo_sc[:, cols] = ua[tm - 8:, :]
        act = (_gelu_tanh(conv) * ub).astype(BF16)
        part = _dot(act, wd_ref[cols, :])
        acc = part if acc is None else acc + part
    halo_out_ref[...] = halo_sc[...]
    out_ref[...] = h1_ref[...] + _rms(acc, gpost_ref[...])


def _ffn(hn, h1, halo_in, prm, tm, tf):
    bsz, t_len, _ = hn.shape
    n_t = t_len // tm
    row = lambda c: pl.BlockSpec((None, tm, c), lambda b, i: (b, i, 0))
    weights = [prm["w_up"], prm["conv_w"], prm["conv_b"], prm["w_down"], prm["g_ffn_post"]]
    return pl.pallas_call(
        functools.partial(_ffn_kernel, tm=tm, tf=tf),
        grid=(bsz, n_t),
        in_specs=[row(D_MODEL), row(D_MODEL), _resident(halo_in)] + [_resident(a) for a in weights],
        out_specs=[row(D_MODEL), pl.BlockSpec((None, None, 8, D_FF), lambda b, i: (b, i, 0, 0))],
        out_shape=[jax.ShapeDtypeStruct((bsz, t_len, D_MODEL), F32),
                   jax.ShapeDtypeStruct((bsz, n_t, 8, D_FF), F32)],
        scratch_shapes=[pltpu.VMEM((8, D_FF), F32)],
        compiler_params=_params("parallel", "arbitrary"),
        name="ffn",
    )(hn, h1, halo_in, *weights)


def _pick(t_len, options):
    for o in options:
        if t_len % o == 0:
            return o
    raise ValueError(f"no tile for sequence length {t_len}")


def _block(h, carry, prm, tab_fn, n_pad=0):
    t_len = h.shape[1]
    keep = (jnp.arange(t_len) >= n_pad)[None, :, None]
    tm_proj = _pick(t_len, (256, 64))
    tm_rwkv = _pick(t_len, (256, 64))
    rows_s5 = _pick(t_len // S5_TOK, (256, 8))
    tm_mix = _pick(t_len, (512, 64))
    tm_ffn = _pick(t_len, (512, 64))
    p, u, gates = _inproj(h, prm["g_pre"], prm["w_in"], tm_proj)
    ya, rwkv_state = _rwkv(p, carry["prev_row"], carry["rwkv_state"], prm, tm_rwkv)
    yb, s5_state = _s5(u, carry["s5_state"], tab_fn(rows_s5), rows_s5)
    h1, hn2 = _mix(h, gates, ya, yb, prm, tm_mix)
    if n_pad:
        hn2 = jnp.where(keep, hn2, jnp.zeros_like(hn2))
    out, halo = _ffn(hn2, h1, carry["halo"], prm, tm_ffn, 1408)
    if n_pad:
        out = jnp.where(keep, out, jnp.zeros_like(out))
    new_carry = {"prev_row": p[0, -1:, :], "rwkv_state": rwkv_state[0], "s5_state": s5_state[0],
                 "halo": halo[0, -1]}
    return out, new_carry


def kernel(x, meta_tokens, norm_mix_pre, norm_mix_post, w_in, rwkv_shift_mu, rwkv_w0, rwkv_w_up, rwkv_a0, rwkv_a_up, rwkv_g_up, rwkv_k_k, rwkv_k_a, rwkv_r_k, rwkv_ln_gain, rwkv_ln_bias, rwkv_w_branch, s5_lambda_re, s5_lambda_im, s5_log_dt, s5_b_re, s5_b_im, s5_c_re, s5_c_im, s5_d, s5_w_glu, w_out, norm_ffn_pre, norm_ffn_post, ffn_w_up, ffn_conv_w, ffn_conv_b, ffn_w_down):
    depth = w_in.shape[0]
    n_pairs = RWKV_WIDTH // LANES
    zeros_lora = jnp.zeros((W_LORA, RWKV_WIDTH), F32)

    h_meta = jnp.concatenate([jnp.zeros((META_PAD - N_META, D_MODEL), x.dtype), meta_tokens.astype(x.dtype)])[None]
    h_main = x
    for l in range(depth):
        vec = lambda a: a[l].reshape(1, -1).astype(F32)
        prm = {
            "g_pre": vec(norm_mix_pre), "g_post": vec(norm_mix_post),
            "g_ffn_pre": vec(norm_ffn_pre), "g_ffn_post": vec(norm_ffn_post),
            "w_in": w_in[l].astype(BF16),
            "mu": vec(rwkv_shift_mu), "w0": vec(rwkv_w0), "a0": vec(rwkv_a0),
            "wup": jnp.concatenate([rwkv_w_up[l].astype(F32), zeros_lora]),
            "aup": jnp.concatenate([zeros_lora, rwkv_a_up[l].astype(F32)]),
            "gup": rwkv_g_up[l].astype(F32),
            "k_k": vec(rwkv_k_k), "k_a": vec(rwkv_k_a), "r_k": vec(rwkv_r_k),
            "ln_g": vec(rwkv_ln_gain), "ln_b": vec(rwkv_ln_bias),
            "w_branch": rwkv_w_branch[l].astype(BF16), "w_glu": s5_w_glu[l].astype(BF16),
            "w_out": w_out[l].astype(BF16), "w_up": ffn_w_up[l].astype(BF16),
            "conv_w": ffn_conv_w[l].astype(F32), "conv_b": vec(ffn_conv_b),
            "w_down": ffn_w_down[l].astype(BF16),
        }
        tab_fn = functools.partial(_s5_tables, s5_lambda_re[l], s5_lambda_im[l], s5_log_dt[l], s5_b_re[l],
                                   s5_b_im[l], s5_c_re[l], s5_c_im[l], s5_d[l])
        zero_carry = {"prev_row": jnp.zeros((1, RWKV_PROJ), F32),
                      "rwkv_state": jnp.zeros((n_pairs, LANES, LANES), F32),
                      "s5_state": jnp.zeros((S5_WIDTH // LANES, 2, S5_BLOCK_STATE), F32),
                      "halo": jnp.zeros((8, D_FF), F32)}
        h_meta, carry = _block(h_meta, zero_carry, prm, tab_fn, n_pad=META_PAD - N_META)
        h_main, _ = _block(h_main, carry, prm, tab_fn)
    return h_main
```

```python
import functools
import math

import jax
import jax.numpy as jnp
from jax import lax
from jax.experimental import pallas as pl
from jax.experimental.pallas import tpu as pltpu

F32 = jnp.float32
BF16 = jnp.bfloat16
HI = lax.Precision.HIGHEST

D_MODEL = 1024
N_META = 16
RWKV_WIDTH = 512
HEAD_DIM = 64
W_LORA = 64
A_LORA = 64
G_LORA = 128
RWKV_PROJ = 3 * RWKV_WIDTH + W_LORA + A_LORA + G_LORA
GN_EPS = 64e-5
S5_WIDTH = 512
S5_GROUP = 16
S5_STATE = 64
D_FF = 2816
RMS_EPS = 1e-6

LANES = 128
CHUNK = 64
CHUNK_GROUP = 4
S5_TOK = 8
S5_BLOCK_STATE = (LANES // S5_GROUP) * S5_STATE
META_PAD = 64
VMEM_LIMIT = 56 * 1024 * 1024


def _dot(a, b, precision=None):
    return jnp.dot(a, b, preferred_element_type=F32, precision=precision)


def _dot_nt(a, b):
    return lax.dot_general(a, b, (((1,), (1,)), ((), ())), preferred_element_type=F32)


def _dot_tn(a, b):
    return lax.dot_general(a, b, (((0,), (0,)), ((), ())), preferred_element_type=F32)


def _bf(x):
    return x.astype(BF16)


def _split(x):
    hi = x.astype(BF16)
    return hi, (x - hi.astype(F32)).astype(BF16)


def _dot_x3(a, b):
    ah, al = _split(a)
    bh, bl = _split(b)
    return _dot(ah, bh) + _dot(al, bh) + _dot(ah, bl)


def _gelu_tanh(x):
    c = math.sqrt(2.0 / math.pi)
    return 0.5 * x * (1.0 + jnp.tanh(c * (x + 0.044715 * (x * x * x))))


def _softplus(x):
    return jnp.maximum(x, 0.0) + jnp.log(1.0 + jnp.exp(-jnp.abs(x)))


def _rms(x, g):
    ms = jnp.mean(x * x, axis=-1, keepdims=True)
    return x * lax.rsqrt(ms + RMS_EPS) * g


def _params(*sem):
    return pltpu.CompilerParams(dimension_semantics=sem, vmem_limit_bytes=VMEM_LIMIT)


def _resident(a):
    return pl.BlockSpec(a.shape, lambda b, i: (0,) * a.ndim, pipeline_mode=pl.Buffered(1))


def _lane_blocks(tm):
    return pl.BlockSpec((S5_WIDTH // LANES, None, tm, LANES), lambda b, i: (0, b, i, 0))


def _inproj_kernel(h_ref, g_ref, w_ref, p_ref, u_ref, gate_ref):
    hn = _rms(h_ref[...], g_ref[...]).astype(BF16)
    p_ref[...] = _dot(hn, w_ref[:, 0:RWKV_PROJ])
    u = _dot(hn, w_ref[:, RWKV_PROJ:RWKV_PROJ + S5_WIDTH])
    for blk in range(S5_WIDTH // LANES):
        u_ref[blk] = u[:, blk * LANES:(blk + 1) * LANES]
    gate_ref[...] = jax.nn.sigmoid(_dot(hn, w_ref[:, RWKV_PROJ + S5_WIDTH:])).astype(BF16)


def _inproj(h, g, w, tm):
    bsz, t_len, _ = h.shape
    n_w = w.shape[1]
    row = lambda c: pl.BlockSpec((None, tm, c), lambda b, i: (b, i, 0))
    full = lambda a: pl.BlockSpec(a.shape, lambda b, i: (0,) * a.ndim)
    return pl.pallas_call(
        _inproj_kernel,
        grid=(bsz, t_len // tm),
        in_specs=[row(D_MODEL), _resident(g), _resident(w)],
        out_specs=[row(RWKV_PROJ), _lane_blocks(tm), row(n_w - RWKV_PROJ - S5_WIDTH)],
        out_shape=[jax.ShapeDtypeStruct((bsz, t_len, RWKV_PROJ), F32),
                   jax.ShapeDtypeStruct((S5_WIDTH // LANES, bsz, t_len, LANES), F32),
                   jax.ShapeDtypeStruct((bsz, t_len, n_w - RWKV_PROJ - S5_WIDTH), BF16)],
        compiler_params=_params("parallel", "parallel"),
        name="inproj",
    )(h, g, w)


def _rwkv_kernel(p_ref, prev_ref, sin_ref, mu_ref, w0_ref, wup_ref, a0_ref, aup_ref, gup_ref,
                 kk_ref, ka_ref, rk_ref, lng_ref, lnb_ref,
                 y_ref, sout_ref,
                 prev_sc, state_sc, at_sc, rt_sc, bh_sc, kh_sc, bl_sc, kl_sc, v_sc, gl_sc,
                 bon_sc, g_sc, rp_sc, yl_sc, phi_sc, psi_sc, *, tm):
    L = CHUNK
    n_chunks = tm // L
    n_pairs = RWKV_WIDTH // LANES
    n2 = 2 * L

    @pl.when(pl.program_id(1) == 0)
    def _():
        prev_sc[...] = prev_ref[...]
        state_sc[...] = sin_ref[...]

    si = lax.broadcasted_iota(jnp.int32, (n2, n2), 0)
    sj = lax.broadcasted_iota(jnp.int32, (n2, n2), 1)
    same_head = (si // L) == (sj // L)
    strict = same_head & ((si % L) > (sj % L))
    incl = same_head & ((si % L) >= (sj % L))
    eye_mask = si == sj
    eye = eye_mask.astype(F32)
    head_sum = same_head.astype(BF16)
    head0 = lax.broadcasted_iota(jnp.int32, (L, LANES), 1) < HEAD_DIM

    p = p_ref[...]
    row = lax.broadcasted_iota(jnp.int32, p.shape, 0)
    ps = jnp.where(row == 0, prev_sc[...], pltpu.roll(p, 1, 0))
    prev_sc[...] = p_ref[pl.ds(tm - 1, 1), :]
    pm = p + (ps - p) * mu_ref[...]
    r = pm[:, 0:RWKV_WIDTH]
    k = pm[:, RWKV_WIDTH:2 * RWKV_WIDTH]
    v = pm[:, 2 * RWKV_WIDTH:3 * RWKV_WIDTH]
    wa = pm[:, 3 * RWKV_WIDTH:3 * RWKV_WIDTH + W_LORA + A_LORA]
    gd = pm[:, 3 * RWKV_WIDTH + W_LORA + A_LORA:]

    w_log = -_softplus(-(w0_ref[...] + _dot_x3(jnp.tanh(wa), wup_ref[...]))) - 0.5
    logw = -jnp.exp(w_log)
    a = jax.nn.sigmoid(a0_ref[...] + _dot_x3(wa, aup_ref[...]))
    g_sc[...] = _dot(_bf(jax.nn.sigmoid(gd)), _bf(gup_ref[...]))
    kx = k * kk_ref[...]
    kx2 = kx * kx
    rk = r * k
    v_sc[...] = v

    li = lax.broadcasted_iota(jnp.int32, (L, L), 0)
    lj = lax.broadcasted_iota(jnp.int32, (L, L), 1)
    ltri = (li >= lj).astype(BF16)
    for j in range(n_pairs):
        cols = slice(j * LANES, (j + 1) * LANES)
        a_j = a[:, cols]
        kkn = kx[:, cols] * lax.rsqrt(jnp.maximum(_dot(_bf(kx2[:, cols]), head_sum), 1e-24))
        scale = 1.0 + (a_j - 1.0) * ka_ref[:, cols]
        kp = k[:, cols] * scale
        b = kkn * a_j
        bon_sc[:, cols] = _dot(_bf(rk[:, cols] * scale * rk_ref[:, cols]), head_sum) * v[:, cols]
        for c in range(n_chunks):
            rows = slice(c * L, (c + 1) * L)
            lw = logw[rows, cols]
            lw_hi, lw_lo = _split(lw)
            cum = _dot(ltri, lw_hi) + _dot(ltri, lw_lo)
            cum_l = jnp.sum(lw, axis=0, keepdims=True)
            at_sc[rows, cols] = -kkn[rows] * jnp.exp(cum - lw)
            rt_sc[rows, cols] = r[rows, cols] * jnp.exp(cum)
            ginv = jnp.exp(-cum)
            bh_sc[rows, cols] = b[rows] * ginv
            kh_sc[rows, cols] = kp[rows] * ginv
            g_l = jnp.exp(cum_l)
            tail = g_l * ginv
            bl_sc[rows, cols] = b[rows] * tail
            kl_sc[rows, cols] = kp[rows] * tail
            gl_sc[c * 8:(c + 1) * 8, cols] = jnp.broadcast_to(g_l, (8, LANES))

    def split_heads(x):
        return jnp.concatenate([jnp.where(head0, x, 0.0), jnp.where(head0, 0.0, x)], axis=0)

    def pick_heads(xs):
        return jnp.where(head0, xs[:L], xs[L:])

    def merge_heads(xs):
        return xs[:L] + xs[L:]

    group = min(CHUNK_GROUP, n_chunks)

    def local_body(cg, carry):
        units = [(cg * group + dc, j) for dc in range(group) for j in range(n_pairs)]
        n_u = range(len(units))
        rws = [pl.ds(pl.multiple_of(c * L, L), L) for c, _ in units]
        cl = [slice(j * LANES, (j + 1) * LANES) for _, j in units]
        at_s = [split_heads(at_sc[rws[i], cl[i]]) for i in n_u]
        rt_s = [split_heads(rt_sc[rws[i], cl[i]]) for i in n_u]
        bl = [_bf(bl_sc[rws[i], cl[i]]) for i in n_u]
        kl = [_bf(kl_sc[rws[i], cl[i]]) for i in n_u]
        vv = [v_sc[rws[i], cl[i]] for i in n_u]
        gl = [gl_sc[pl.ds(pl.multiple_of(units[i][0] * 8, 8), 1), cl[i]] for i in n_u]
        bk_s = [_bf(jnp.concatenate([split_heads(bh_sc[rws[i], cl[i]]), split_heads(kh_sc[rws[i], cl[i]])],
                                    axis=0)) for i in n_u]
        at_b = [_bf(x) for x in at_s]
        lhs = [jnp.concatenate([at_b[i], _bf(rt_s[i])], axis=0) for i in n_u]
        abk = [_dot_nt(lhs[i], bk_s[i]) for i in n_u]
        a_ab = [jnp.where(strict, abk[i][:n2, :n2], 0.0) for i in n_u]
        a_ak = [_bf(jnp.where(strict, abk[i][:n2, n2:], 0.0)) for i in n_u]
        a_rb = [_bf(jnp.where(incl, abk[i][n2:, :n2], 0.0)) for i in n_u]
        a_rk = [_bf(jnp.where(incl, abk[i][n2:, n2:], 0.0)) for i in n_u]
        first = ((si // 2) == (sj // 2)) & ((si % 2) == 1) & ((sj % 2) == 0)
        tinv = [eye + jnp.where(first, a_ab[i], 0.0) for i in n_u]
        s = 2
        while s < L:
            off = ((si // (2 * s)) == (sj // (2 * s))) & ((si % (2 * s)) >= s) & ((sj % (2 * s)) < s)
            t_b = [_bf(t) for t in tinv]
            x1 = [_dot(_bf(jnp.where(off, a_ab[i], 0.0)), t_b[i]) for i in n_u]
            x2 = [_dot(t_b[i], _bf(x1[i])) for i in n_u]
            tinv = [tinv[i] + x2[i] for i in n_u]
            s *= 2
        v2 = [_bf(jnp.concatenate([vv[i], vv[i]], axis=0)) for i in n_u]
        av = [_dot(a_ak[i], v2[i]) for i in n_u]
        tx = [_dot(_bf(tinv[i]), jnp.concatenate([at_b[i], _bf(av[i])], axis=1)) for i in n_u]
        g1 = [_dot(a_rb[i], _bf(tx[i])) for i in n_u]
        g2 = [_dot(a_rk[i], v2[i]) for i in n_u]
        w = [_bf(merge_heads(tx[i][:, :LANES])) for i in n_u]
        uv = [_bf(jnp.concatenate([pick_heads(tx[i][:, LANES:]), vv[i]], axis=0)) for i in n_u]
        phi = [_dot_tn(w[i], bl[i]) for i in n_u]
        psi = [_dot_tn(uv[i], jnp.concatenate([bl[i], kl[i]], axis=0)) for i in n_u]
        for i in n_u:
            c, j = units[i]
            rp_sc[rws[i], cl[i]] = merge_heads(rt_s[i] + g1[i][:, :LANES])
            yl_sc[rws[i], cl[i]] = pick_heads(g1[i][:, LANES:] + g2[i])
            phi_sc[c, j] = jnp.where(same_head, phi[i], 0.0) + jnp.where(eye_mask, gl[i], 0.0)
            psi_sc[c, j] = jnp.where(same_head, psi[i], 0.0)
        return carry

    lax.fori_loop(0, n_chunks // group, local_body, 0)

    def state_body(c, carry):
        rws = pl.ds(pl.multiple_of(c * L, L), L)
        pairs = range(n_pairs)
        cl = [slice(j * LANES, (j + 1) * LANES) for j in pairs]
        st = [_split(state_sc[j]) for j in pairs]
        ph = [_split(phi_sc[c, j]) for j in pairs]
        rp = [_bf(rp_sc[rws, cl[j]]) for j in pairs]
        y0 = [_dot_nt(rp[j], st[j][0]) for j in pairs]
        n0 = [_dot(st[j][0], ph[j][0]) for j in pairs]
        n1 = [_dot(st[j][1], ph[j][0]) for j in pairs]
        n2_ = [_dot(st[j][0], ph[j][1]) for j in pairs]
        for j in pairs:
            rp_sc[rws, cl[j]] = y0[j] + yl_sc[rws, cl[j]]
            state_sc[j] = n0[j] + n1[j] + n2_[j] + psi_sc[c, j]
        return carry

    lax.fori_loop(0, n_chunks, state_body, 0)

    inv_n = 1.0 / HEAD_DIM
    for j in range(n_pairs):
        cols = slice(j * LANES, (j + 1) * LANES)
        y = rp_sc[:, cols]
        yc = y - _dot(_bf(y), head_sum) * inv_n
        var = _dot(_bf(yc * yc), head_sum) * inv_n
        yn = yc * lax.rsqrt(var + GN_EPS) * lng_ref[:, cols] + lnb_ref[:, cols]
        y_ref[:, cols] = ((yn + bon_sc[:, cols]) * g_sc[:, cols]).astype(BF16)
    sout_ref[...] = state_sc[...]


def _rwkv(p, prev_row, state_in, prm, tm):
    bsz, t_len, _ = p.shape
    n_pairs = RWKV_WIDTH // LANES
    n_chunks = tm // CHUNK
    full = lambda a: pl.BlockSpec(a.shape, lambda b, i: (0,) * a.ndim)
    weights = [prm["mu"], prm["w0"], prm["wup"], prm["a0"], prm["aup"], prm["gup"], prm["k_k"], prm["k_a"],
               prm["r_k"], prm["ln_g"], prm["ln_b"]]
    tile = pltpu.VMEM((tm, RWKV_WIDTH), F32)
    mats = pltpu.VMEM((n_chunks, n_pairs, LANES, LANES), F32)
    return pl.pallas_call(
        functools.partial(_rwkv_kernel, tm=tm),
        grid=(bsz, t_len // tm),
        in_specs=[pl.BlockSpec((None, tm, RWKV_PROJ), lambda b, i: (b, i, 0)), full(prev_row), full(state_in)]
                 + [full(a) for a in weights],
        out_specs=[pl.BlockSpec((None, tm, RWKV_WIDTH), lambda b, i: (b, i, 0)),
                   pl.BlockSpec((None, n_pairs, LANES, LANES), lambda b, i: (b, 0, 0, 0))],
        out_shape=[jax.ShapeDtypeStruct((bsz, t_len, RWKV_WIDTH), BF16),
                   jax.ShapeDtypeStruct((bsz, n_pairs, LANES, LANES), F32)],
        scratch_shapes=[pltpu.VMEM((1, RWKV_PROJ), F32), pltpu.VMEM((n_pairs, LANES, LANES), F32)]
                       + [tile] * 7 + [pltpu.VMEM((n_chunks * 8, RWKV_WIDTH), F32)] + [tile] * 4 + [mats] * 2,
        compiler_params=_params("parallel", "arbitrary"),
        name="rwkv",
    )(p, prev_row, state_in, *weights)


def _s5_kernel(u_ref, sin_ref, m_ref, q_ref, p_ref, lr_ref, li_ref, d_ref,
               y_ref, sout_ref, st_sc, *, rows):
    half = S5_BLOCK_STATE

    @pl.when(pl.program_id(2) == 0)
    def _():
        st_sc[...] = sin_ref[...]

    u = jnp.concatenate([u_ref[pl.ds(t, rows, stride=S5_TOK), :] for t in range(S5_TOK)], axis=1)
    ub = u.astype(BF16)
    z = _dot(ub, q_ref[...])
    y_in = _dot(ub, m_ref[...]) + d_ref[...] * u

    row = lax.broadcasted_iota(jnp.int32, (rows, half), 0)
    sub = row % 8
    xr = z[:, :half]
    xi = z[:, half:]
    s = 1
    while s < min(8, rows):
        ar = lr_ref[s - 1:s, :]
        ai = li_ref[s - 1:s, :]
        pr = jnp.where(sub >= s, pltpu.roll(xr, s, 0), 0.0)
        pi = jnp.where(sub >= s, pltpu.roll(xi, s, 0), 0.0)
        xr, xi = xr + (ar * pr - ai * pi), xi + (ar * pi + ai * pr)
        s *= 2
    tr = lr_ref[...]
    ti = li_ref[...]
    cr = st_sc[0:1, :]
    ci = st_sc[1:2, :]
    out_r, out_i = [], []
    for blk in range(rows // 8):
        br = xr[blk * 8:(blk + 1) * 8] + (tr * cr - ti * ci)
        bi = xi[blk * 8:(blk + 1) * 8] + (tr * ci + ti * cr)
        out_r.append(br)
        out_i.append(bi)
        cr = br[7:8]
        ci = bi[7:8]
    xr = jnp.concatenate(out_r, axis=0)
    xi = jnp.concatenate(out_i, axis=0)
    sr = st_sc[0:1, :]
    si = st_sc[1:2, :]
    st_sc[0:1, :] = cr
    st_sc[1:2, :] = ci
    sout_ref[...] = st_sc[...]
    x0 = jnp.concatenate([jnp.where(row == 0, sr, pltpu.roll(xr, 1, 0)),
                          jnp.where(row == 0, si, pltpu.roll(xi, 1, 0))], axis=1)
    ya = _gelu_tanh(y_in + _dot(_bf(x0), p_ref[...]))
    for t in range(S5_TOK):
        y_ref[pl.ds(t, rows, stride=S5_TOK), :] = ya[:, t * LANES:(t + 1) * LANES]


def _s5(u, state_in, tab, rows):
    n_blk, bsz, t_len, _ = u.shape
    tm = rows * S5_TOK
    blk = lambda a: pl.BlockSpec((None,) + a.shape[1:], lambda b, k, i: (k,) + (0,) * (a.ndim - 1))
    tokens = pl.BlockSpec((None, None, tm, LANES), lambda b, k, i: (k, b, i, 0))
    weights = [tab["m"], tab["q"], tab["p"], tab["lr"], tab["li"], tab["d"]]
    return pl.pallas_call(
        functools.partial(_s5_kernel, rows=rows),
        grid=(bsz, n_blk, t_len // tm),
        in_specs=[tokens, blk(state_in)] + [blk(a) for a in weights],
        out_specs=[tokens, pl.BlockSpec((None, None, 2, S5_BLOCK_STATE), lambda b, k, i: (b, k, 0, 0))],
        out_shape=[jax.ShapeDtypeStruct(u.shape, F32),
                   jax.ShapeDtypeStruct((bsz, n_blk, 2, S5_BLOCK_STATE), F32)],
        scratch_shapes=[pltpu.VMEM((2, S5_BLOCK_STATE), F32)],
        compiler_params=_params("parallel", "arbitrary", "arbitrary"),
        name="s5",
    )(u, state_in, *weights)


def _s5_tables(lam_re, lam_im, log_dt, b_re, b_im, c_re, c_im, d):
    f = lambda a: a.astype(F32)
    re = jnp.minimum(f(lam_re), -1e-4)
    im = f(lam_im)
    dt = jnp.exp(f(log_dt))[:, None]

    def power(m):
        mag = jnp.exp(m * (re * dt))
        ang = m * (im * dt)
        return mag * jnp.cos(ang), mag * jnp.sin(ang)

    l1r, l1i = power(1.0)
    den = re * re + im * im
    cr = ((l1r - 1.0) * re + l1i * im) / den
    ci = (l1i * re - (l1r - 1.0) * im) / den
    bbr = cr[..., None] * f(b_re) - ci[..., None] * f(b_im)
    bbi = cr[..., None] * f(b_im) + ci[..., None] * f(b_re)
    pw = [power(float(m)) for m in range(S5_TOK + 1)]
    pwr = jnp.stack([x[0] for x in pw])
    pwi = jnp.stack([x[1] for x in pw])
    ccr, cci = f(c_re), f(c_im)
    n_blk = S5_WIDTH // LANES
    gpb = LANES // S5_GROUP
    big = S5_TOK * LANES

    def widen(src, row_grp, col_grp, sub):
        ci = lax.broadcasted_iota(jnp.int32, (LANES, big), 0)
        cj = lax.broadcasted_iota(jnp.int32, (LANES, big), 1)
        spread = ((ci // sub == cj // (sub * gpb)) & (ci % sub == cj % sub)).astype(BF16)
        ri = lax.broadcasted_iota(jnp.int32, (big, big), 0)
        rj = lax.broadcasted_iota(jnp.int32, (big, big), 1)
        keep = ((ri // row_grp) % gpb == (rj // col_grp) % gpb).astype(F32)
        wide = jnp.einsum("brk,kc->brc", src.astype(BF16), spread, preferred_element_type=F32)
        return (wide * keep[None]).astype(BF16)

    pbr = pwr[:S5_TOK, :, :, None] * bbr[None] - pwi[:S5_TOK, :, :, None] * bbi[None]
    pbi = pwr[:S5_TOK, :, :, None] * bbi[None] + pwi[:S5_TOK, :, :, None] * bbr[None]
    kern = (jnp.einsum("gon,mgnh->mgoh", ccr, pbr, precision=HI)
            - jnp.einsum("gon,mgnh->mgoh", cci, pbi, precision=HI))
    ti = jnp.arange(S5_TOK)[:, None]
    to = jnp.arange(S5_TOK)[None, :]
    diff = to - ti
    toep = jnp.where((diff >= 0)[:, :, None, None, None], kern[jnp.clip(diff, 0, S5_TOK - 1)], 0.0)
    m6 = toep.reshape(S5_TOK, S5_TOK, n_blk, gpb, S5_GROUP, S5_GROUP).transpose(2, 0, 3, 5, 1, 4)
    m_big = widen(m6.reshape(n_blk, big, LANES), S5_GROUP, S5_GROUP, S5_GROUP)

    q2 = jnp.stack([pbr[::-1], pbi[::-1]])
    q6 = q2.reshape(2, S5_TOK, n_blk, gpb, S5_STATE, S5_GROUP).transpose(2, 1, 3, 5, 0, 4)
    q_big = widen(q6.reshape(n_blk, big, LANES), S5_GROUP, S5_STATE, S5_STATE)
    cpr = ccr[None] * pwr[1:, :, None, :] - cci[None] * pwi[1:, :, None, :]
    cpi = ccr[None] * pwi[1:, :, None, :] + cci[None] * pwr[1:, :, None, :]
    p2 = jnp.stack([cpr, -cpi])
    p6 = p2.reshape(2, S5_TOK, n_blk, gpb, S5_GROUP, S5_STATE).transpose(2, 0, 3, 5, 1, 4)
    p_big = widen(p6.reshape(n_blk, big, LANES), S5_STATE, S5_GROUP, S5_GROUP)

    lv = [power(float(S5_TOK * (rr + 1))) for rr in range(8)]
    lr = jnp.stack([x[0].reshape(n_blk, -1) for x in lv], axis=1)
    li = jnp.stack([x[1].reshape(n_blk, -1) for x in lv], axis=1)
    d_cols = jnp.broadcast_to(f(d).reshape(n_blk, 1, 1, LANES), (n_blk, 1, S5_TOK, LANES)).reshape(n_blk, 1, -1)
    return {"m": m_big, "q": q_big, "p": p_big, "lr": lr, "li": li, "d": d_cols}


def _mix_kernel(h_ref, gate_ref, ya_ref, yb_ref, wbr_ref, wglu_ref, wout_ref, gpost_ref, gpre_ref,
                h1_ref, hn_ref):
    y_a = _dot(ya_ref[...], wbr_ref[...])
    yb = jnp.concatenate([yb_ref[blk] for blk in range(S5_WIDTH // LANES)], axis=1)
    z = _dot(yb.astype(BF16), wglu_ref[...])
    y_b = z[:, :D_MODEL] * jax.nn.sigmoid(z[:, D_MODEL:])
    gates = gate_ref[...].astype(F32)
    mix = (gates[:, :D_MODEL] * y_a + gates[:, D_MODEL:] * y_b).astype(BF16)
    h1 = h_ref[...] + _rms(_dot(mix, wout_ref[...]), gpost_ref[...])
    h1_ref[...] = h1
    hn_ref[...] = _rms(h1, gpre_ref[...]).astype(BF16)


def _mix(h, gates, ya, yb, prm, tm):
    bsz, t_len, _ = h.shape
    row = lambda c: pl.BlockSpec((None, tm, c), lambda b, i: (b, i, 0))
    full = lambda a: pl.BlockSpec(a.shape, lambda b, i: (0,) * a.ndim)
    weights = [prm["w_branch"], prm["w_glu"], prm["w_out"], prm["g_post"], prm["g_ffn_pre"]]
    return pl.pallas_call(
        _mix_kernel,
        grid=(bsz, t_len // tm),
        in_specs=[row(D_MODEL), row(2 * D_MODEL), row(RWKV_WIDTH), _lane_blocks(tm)] + [_resident(a) for a in weights],
        out_specs=[row(D_MODEL), row(D_MODEL)],
        out_shape=[jax.ShapeDtypeStruct((bsz, t_len, D_MODEL), F32),
                   jax.ShapeDtypeStruct((bsz, t_len, D_MODEL), BF16)],
        compiler_params=_params("parallel", "parallel"),
        name="mix",
    )(h, gates, ya, yb, *weights)


def _ffn_kernel(hn_ref, h1_ref, halo_ref, wup_ref, cw_ref, cb_ref, wd_ref, gpost_ref,
                out_ref, halo_out_ref, halo_sc, *, tm, tf):
    @pl.when(pl.program_id(1) == 0)
    def _():
        halo_sc[...] = halo_ref[...]

    hn = hn_ref[...]
    row = lax.broadcasted_iota(jnp.int32, (tm, tf), 0)
    acc = None
    for f in range(D_FF // tf):
        cols = slice(f * tf, (f + 1) * tf)
        ua = _dot(hn, wup_ref[:, cols])
        ub = _dot(hn, wup_ref[:, D_FF + f * tf:D_FF + (f + 1) * tf])
        prev1 = halo_sc[7:8, cols]
        prev2 = halo_sc[6:7, cols]
        u1 = jnp.where(row == 0, prev1, pltpu.roll(ua, 1, 0))
        u2 = jnp.where(row == 0, prev2, jnp.where(row == 1, prev1, pltpu.roll(ua, 2, 0)))
        conv = cw_ref[0:1, cols] * u2 + cw_ref[1:2, cols] * u1 + cw_ref[2:3, cols] * ua + cb_ref[:, cols]
        halo_sc[:, cols] = ua[tm - 8:, :]
        act = (_gelu_tanh(conv) * ub).astype(BF16)
        part = _dot(act, wd_ref[cols, :])
        acc = part if acc is None else acc + part
    halo_out_ref[...] = halo_sc[...]
    out_ref[...] = h1_ref[...] + _rms(acc, gpost_ref[...])


def _ffn(hn, h1, halo_in, prm, tm, tf):
    bsz, t_len, _ = hn.shape
    n_t = t_len // tm
    row = lambda c: pl.BlockSpec((None, tm, c), lambda b, i: (b, i, 0))
    weights = [prm["w_up"], prm["conv_w"], prm["conv_b"], prm["w_down"], prm["g_ffn_post"]]
    return pl.pallas_call(
        functools.partial(_ffn_kernel, tm=tm, tf=tf),
        grid=(bsz, n_t),
        in_specs=[row(D_MODEL), row(D_MODEL), _resident(halo_in)] + [_resident(a) for a in weights],
        out_specs=[row(D_MODEL), pl.BlockSpec((None, None, 8, D_FF), lambda b, i: (b, i, 0, 0))],
        out_shape=[jax.ShapeDtypeStruct((bsz, t_len, D_MODEL), F32),
                   jax.ShapeDtypeStruct((bsz, n_t, 8, D_FF), F32)],
        scratch_shapes=[pltpu.VMEM((8, D_FF), F32)],
        compiler_params=_params("parallel", "arbitrary"),
        name="ffn",
    )(hn, h1, halo_in, *weights)


def _pick(t_len, options):
    for o in options:
        if t_len % o == 0:
            return o
    raise ValueError(f"no tile for sequence length {t_len}")


def _block(h, carry, prm, tab, n_pad=0):
    t_len = h.shape[1]
    keep = (jnp.arange(t_len) >= n_pad)[None, :, None]
    tm_proj = _pick(t_len, (256, 64))
    tm_rwkv = _pick(t_len, (256, 64))
    rows_s5 = _pick(t_len // S5_TOK, (256, 8))
    tm_mix = _pick(t_len, (1024, 64))
    tm_ffn = _pick(t_len, (512, 64))
    p, u, gates = _inproj(h, prm["g_pre"], prm["w_in"], tm_proj)
    ya, rwkv_state = _rwkv(p, carry["prev_row"], carry["rwkv_state"], prm, tm_rwkv)
    yb, s5_state = _s5(u, carry["s5_state"], tab, rows_s5)
    h1, hn2 = _mix(h, gates, ya, yb, prm, tm_mix)
    if n_pad:
        hn2 = jnp.where(keep, hn2, jnp.zeros_like(hn2))
    out, halo = _ffn(hn2, h1, carry["halo"], prm, tm_ffn, 1408)
    if n_pad:
        out = jnp.where(keep, out, jnp.zeros_like(out))
    new_carry = {"prev_row": p[0, -1:, :], "rwkv_state": rwkv_state[0], "s5_state": s5_state[0],
                 "halo": halo[0, -1]}
    return out, new_carry


def kernel(x, meta_tokens, norm_mix_pre, norm_mix_post, w_in, rwkv_shift_mu, rwkv_w0, rwkv_w_up, rwkv_a0, rwkv_a_up, rwkv_g_up, rwkv_k_k, rwkv_k_a, rwkv_r_k, rwkv_ln_gain, rwkv_ln_bias, rwkv_w_branch, s5_lambda_re, s5_lambda_im, s5_log_dt, s5_b_re, s5_b_im, s5_c_re, s5_c_im, s5_d, s5_w_glu, w_out, norm_ffn_pre, norm_ffn_post, ffn_w_up, ffn_conv_w, ffn_conv_b, ffn_w_down):
    depth = w_in.shape[0]
    n_pairs = RWKV_WIDTH // LANES
    zeros_lora = jnp.zeros((W_LORA, RWKV_WIDTH), F32)

    h_meta = jnp.concatenate([jnp.zeros((META_PAD - N_META, D_MODEL), x.dtype), meta_tokens.astype(x.dtype)])[None]
    h_main = x
    for l in range(depth):
        vec = lambda a: a[l].reshape(1, -1).astype(F32)
        prm = {
            "g_pre": vec(norm_mix_pre), "g_post": vec(norm_mix_post),
            "g_ffn_pre": vec(norm_ffn_pre), "g_ffn_post": vec(norm_ffn_post),
            "w_in": w_in[l].astype(BF16),
            "mu": vec(rwkv_shift_mu), "w0": vec(rwkv_w0), "a0": vec(rwkv_a0),
            "wup": jnp.concatenate([rwkv_w_up[l].astype(F32), zeros_lora]),
            "aup": jnp.concatenate([zeros_lora, rwkv_a_up[l].astype(F32)]),
            "gup": rwkv_g_up[l].astype(F32),
            "k_k": vec(rwkv_k_k), "k_a": vec(rwkv_k_a), "r_k": vec(rwkv_r_k),
            "ln_g": vec(rwkv_ln_gain), "ln_b": vec(rwkv_ln_bias),
            "w_branch": rwkv_w_branch[l].astype(BF16), "w_glu": s5_w_glu[l].astype(BF16),
            "w_out": w_out[l].astype(BF16), "w_up": ffn_w_up[l].astype(BF16),
            "conv_w": ffn_conv_w[l].astype(F32), "conv_b": vec(ffn_conv_b),
            "w_down": ffn_w_down[l].astype(BF16),
        }
        tab = _s5_tables(s5_lambda_re[l], s5_lambda_im[l], s5_log_dt[l], s5_b_re[l],
                         s5_b_im[l], s5_c_re[l], s5_c_im[l], s5_d[l])
        zero_carry = {"prev_row": jnp.zeros((1, RWKV_PROJ), F32),
                      "rwkv_state": jnp.zeros((n_pairs, LANES, LANES), F32),
                      "s5_state": jnp.zeros((S5_WIDTH // LANES, 2, S5_BLOCK_STATE), F32),
                      "halo": jnp.zeros((8, D_FF), F32)}
        h_meta, carry = _block(h_meta, zero_carry, prm, tab, n_pad=META_PAD - N_META)
        h_main, _ = _block(h_main, carry, prm, tab)
    return h_main
```

```python
import functools
import math

import jax
import jax.numpy as jnp
from jax import lax
from jax.experimental import pallas as pl
from jax.experimental.pallas import tpu as pltpu

F32 = jnp.float32
BF16 = jnp.bfloat16
HI = lax.Precision.HIGHEST

D_MODEL = 1024
N_META = 16
RWKV_WIDTH = 512
HEAD_DIM = 64
W_LORA = 64
A_LORA = 64
G_LORA = 128
RWKV_PROJ = 3 * RWKV_WIDTH + W_LORA + A_LORA + G_LORA
GN_EPS = 64e-5
S5_WIDTH = 512
S5_GROUP = 16
S5_STATE = 64
D_FF = 2816
RMS_EPS = 1e-6

LANES = 128
CHUNK = 64
CHUNK_GROUP = 4
S5_TOK = 8
S5_BLOCK_STATE = (LANES // S5_GROUP) * S5_STATE
META_PAD = 64
VMEM_LIMIT = 56 * 1024 * 1024


def _dot(a, b, precision=None):
    return jnp.dot(a, b, preferred_element_type=F32, precision=precision)


def _dot_nt(a, b):
    return lax.dot_general(a, b, (((1,), (1,)), ((), ())), preferred_element_type=F32)


def _dot_tn(a, b):
    return lax.dot_general(a, b, (((0,), (0,)), ((), ())), preferred_element_type=F32)


def _bf(x):
    return x.astype(BF16)


def _split(x):
    hi = x.astype(BF16)
    return hi, (x - hi.astype(F32)).astype(BF16)


def _dot_x3(a, b):
    ah, al = _split(a)
    bh, bl = _split(b)
    return _dot(ah, bh) + _dot(al, bh) + _dot(ah, bl)


def _gelu_tanh(x):
    c = math.sqrt(2.0 / math.pi)
    return 0.5 * x * (1.0 + jnp.tanh(c * (x + 0.044715 * (x * x * x))))


def _softplus(x):
    return jnp.maximum(x, 0.0) + jnp.log(1.0 + jnp.exp(-jnp.abs(x)))


def _rms(x, g):
    ms = jnp.mean(x * x, axis=-1, keepdims=True)
    return x * lax.rsqrt(ms + RMS_EPS) * g


def _params(*sem):
    return pltpu.CompilerParams(dimension_semantics=sem, vmem_limit_bytes=VMEM_LIMIT)


def _resident(a):
    return pl.BlockSpec(a.shape, lambda b, i: (0,) * a.ndim, pipeline_mode=pl.Buffered(1))


def _lane_blocks(tm):
    return pl.BlockSpec((S5_WIDTH // LANES, None, tm, LANES), lambda b, i: (0, b, i, 0))


def _inproj_kernel(h_ref, prev_ref, g_ref, w_ref, mu_ref, w0_ref, wup_ref, a0_ref, aup_ref, gup_ref,
                   logw_ref, r_ref, k_ref, v_ref, a_ref, og_ref, u_ref, gate_ref, last_ref,
                   prev_sc, *, tm):
    @pl.when(pl.program_id(1) == 0)
    def _():
        prev_sc[...] = prev_ref[...]

    n_sub = 2 if tm % 512 == 0 else 1
    sub = tm // n_sub
    rws = [slice(q * sub, (q + 1) * sub) for q in range(n_sub)]
    subs = range(n_sub)
    hn = [_rms(h_ref[r, :], g_ref[...]).astype(BF16) for r in rws]
    n_gate = gate_ref.shape[-1]
    gate_cols = n_gate // 4

    def gate_part(q):
        lo = RWKV_PROJ + S5_WIDTH + q * gate_cols
        for i in subs:
            gate_ref[rws[i], q * gate_cols:(q + 1) * gate_cols] = jax.nn.sigmoid(
                _dot(hn[i], w_ref[:, lo:lo + gate_cols])).astype(BF16)

    p = [_dot(x, w_ref[:, 0:RWKV_PROJ]) for x in hn]
    row = lax.broadcasted_iota(jnp.int32, p[0].shape, 0)
    lasts = [prev_sc[...]] + [x[sub - 1:, :] for x in p]
    prev_sc[...] = lasts[n_sub]
    last_ref[...] = lasts[n_sub]
    pm = []
    for i in subs:
        ps = jnp.where(row == 0, lasts[i], pltpu.roll(p[i], 1, 0))
        pm.append(p[i] + (ps - p[i]) * mu_ref[...])
    for i in subs:
        r_ref[rws[i], :] = pm[i][:, 0:RWKV_WIDTH].astype(BF16)
        k_ref[rws[i], :] = pm[i][:, RWKV_WIDTH:2 * RWKV_WIDTH].astype(BF16)
        v_ref[rws[i], :] = pm[i][:, 2 * RWKV_WIDTH:3 * RWKV_WIDTH].astype(BF16)
    gate_part(0)
    wa = [x[:, 3 * RWKV_WIDTH:3 * RWKV_WIDTH + W_LORA + A_LORA] for x in pm]
    gd = [x[:, 3 * RWKV_WIDTH + W_LORA + A_LORA:] for x in pm]
    for i in subs:
        w_log = -_softplus(-(w0_ref[...] + _dot_x3(jnp.tanh(wa[i]), wup_ref[...]))) - 0.5
        logw_ref[rws[i], :] = -jnp.exp(w_log)
    gate_part(1)
    for i in subs:
        a_ref[rws[i], :] = jax.nn.sigmoid(a0_ref[...] + _dot_x3(wa[i], aup_ref[...])).astype(BF16)
        og_ref[rws[i], :] = _dot(_bf(jax.nn.sigmoid(gd[i])), _bf(gup_ref[...])).astype(BF16)
    gate_part(2)
    gate_part(3)
    for i in subs:
        u = _dot(hn[i], w_ref[:, RWKV_PROJ:RWKV_PROJ + S5_WIDTH])
        for blk in range(S5_WIDTH // LANES):
            u_ref[blk, rws[i], :] = u[:, blk * LANES:(blk + 1) * LANES]


def _inproj(h, prev_row, prm, tm):
    bsz, t_len, _ = h.shape
    n_t = t_len // tm
    w = prm["w_in"]
    n_gate = w.shape[1] - RWKV_PROJ - S5_WIDTH
    row = lambda c: pl.BlockSpec((None, tm, c), lambda b, i: (b, i, 0))
    weights = [prm["g_pre"], w, prm["mu"], prm["w0"], prm["wup"], prm["a0"], prm["aup"], prm["gup"]]
    act = lambda dt: jax.ShapeDtypeStruct((bsz, t_len, RWKV_WIDTH), dt)
    return pl.pallas_call(
        functools.partial(_inproj_kernel, tm=tm),
        grid=(bsz, n_t),
        in_specs=[row(D_MODEL), _resident(prev_row)] + [_resident(a) for a in weights],
        out_specs=[row(RWKV_WIDTH)] * 6 + [_lane_blocks(tm), row(n_gate),
                                           pl.BlockSpec((None, None, 1, RWKV_PROJ), lambda b, i: (b, i, 0, 0))],
        out_shape=[act(F32)] + [act(BF16)] * 5
                  + [jax.ShapeDtypeStruct((S5_WIDTH // LANES, bsz, t_len, LANES), F32),
                     jax.ShapeDtypeStruct((bsz, t_len, n_gate), BF16),
                     jax.ShapeDtypeStruct((bsz, n_t, 1, RWKV_PROJ), F32)],
        scratch_shapes=[pltpu.VMEM((1, RWKV_PROJ), F32)],
        compiler_params=_params("parallel", "arbitrary"),
        name="inproj",
    )(h, prev_row, *weights)


def _rwkv_kernel(logw_ref, r_ref, k_ref, v_ref, a_ref, og_ref, sin_ref, kk_ref, ka_ref, rk_ref, lng_ref, lnb_ref,
                 y_ref, sout_ref,
                 state_sc, at_sc, rt_sc, bh_sc, kh_sc, bl_sc, kl_sc, gl_sc, bon_sc, rp_sc, yl_sc, phi_sc, psi_sc, *, tm):
    n_b = y_ref.shape[0]
    L = CHUNK
    n_chunks = tm // L
    n_pairs = RWKV_WIDTH // LANES
    n2 = 2 * L

    @pl.when(pl.program_id(1) == 0)
    def _():
        for bi in range(n_b):
            state_sc[bi] = sin_ref[...]

    si = lax.broadcasted_iota(jnp.int32, (n2, n2), 0)
    sj = lax.broadcasted_iota(jnp.int32, (n2, n2), 1)
    same_head = (si // L) == (sj // L)
    strict = same_head & ((si % L) > (sj % L))
    incl = same_head & ((si % L) >= (sj % L))
    eye_mask = si == sj
    eye = eye_mask.astype(F32)
    head_sum = same_head.astype(BF16)
    head0 = lax.broadcasted_iota(jnp.int32, (L, LANES), 1) < HEAD_DIM

    li = lax.broadcasted_iota(jnp.int32, (L, L), 0)
    lj = lax.broadcasted_iota(jnp.int32, (L, L), 1)
    ltri = (li >= lj).astype(BF16)
    for bi, j in [(bi, j) for bi in range(n_b) for j in range(n_pairs)]:
        cols = slice(j * LANES, (j + 1) * LANES)
        k_j = k_ref[bi, :, cols].astype(F32)
        a_j = a_ref[bi, :, cols].astype(F32)
        r_j = r_ref[bi, :, cols].astype(F32)
        kx = k_j * kk_ref[:, cols]
        kkn_j = kx * lax.rsqrt(jnp.maximum(_dot(_bf(kx * kx), head_sum), 1e-24))
        scale = 1.0 + (a_j - 1.0) * ka_ref[:, cols]
        kp_j = k_j * scale
        b_j = kkn_j * a_j
        bon_sc[bi, :, cols] = (_dot(_bf(r_j * kp_j * rk_ref[:, cols]), head_sum)
                               * v_ref[bi, :, cols].astype(F32))
        for c in range(n_chunks):
            rows = slice(c * L, (c + 1) * L)
            lw = logw_ref[bi, rows, cols]
            lw_hi, lw_lo = _split(lw)
            cum = _dot(ltri, lw_hi) + _dot(ltri, lw_lo)
            cum_l = jnp.sum(lw, axis=0, keepdims=True)
            b = b_j[rows]
            kp = kp_j[rows]
            at_sc[bi, rows, cols] = -kkn_j[rows] * jnp.exp(cum - lw)
            rt_sc[bi, rows, cols] = r_j[rows] * jnp.exp(cum)
            ginv = jnp.exp(-cum)
            bh_sc[bi, rows, cols] = b * ginv
            kh_sc[bi, rows, cols] = kp * ginv
            g_l = jnp.exp(cum_l)
            tail = g_l * ginv
            bl_sc[bi, rows, cols] = b * tail
            kl_sc[bi, rows, cols] = kp * tail
            gl_sc[bi, c * 8:(c + 1) * 8, cols] = jnp.broadcast_to(g_l, (8, LANES))

    def split_heads(x):
        return jnp.concatenate([jnp.where(head0, x, 0.0), jnp.where(head0, 0.0, x)], axis=0)

    def pick_heads(xs):
        return jnp.where(head0, xs[:L], xs[L:])

    def merge_heads(xs):
        return xs[:L] + xs[L:]

    group = min(CHUNK_GROUP, n_chunks)

    def local_body(cg, bi):
        units = [(cg * group + dc, j) for dc in range(group) for j in range(n_pairs)]
        n_u = range(len(units))
        rws = [pl.ds(pl.multiple_of(c * L, L), L) for c, _ in units]
        cl = [slice(j * LANES, (j + 1) * LANES) for _, j in units]
        at_s = [split_heads(at_sc[bi, rws[i], cl[i]]) for i in n_u]
        rt_s = [split_heads(rt_sc[bi, rws[i], cl[i]]) for i in n_u]
        bl = [_bf(bl_sc[bi, rws[i], cl[i]]) for i in n_u]
        kl = [_bf(kl_sc[bi, rws[i], cl[i]]) for i in n_u]
        vv = [v_ref[bi, rws[i], cl[i]] for i in n_u]
        gl = [gl_sc[bi, pl.ds(pl.multiple_of(units[i][0] * 8, 8), 1), cl[i]] for i in n_u]
        bk_s = [_bf(jnp.concatenate([split_heads(bh_sc[bi, rws[i], cl[i]]), split_heads(kh_sc[bi, rws[i], cl[i]])],
                                    axis=0)) for i in n_u]
        at_b = [_bf(x) for x in at_s]
        lhs = [jnp.concatenate([at_b[i], _bf(rt_s[i])], axis=0) for i in n_u]
        abk = [_dot_nt(lhs[i], bk_s[i]) for i in n_u]
        a_ab = [jnp.where(strict, abk[i][:n2, :n2], 0.0) for i in n_u]
        a_ak = [_bf(jnp.where(strict, abk[i][:n2, n2:], 0.0)) for i in n_u]
        a_rb = [_bf(jnp.where(incl, abk[i][n2:, :n2], 0.0)) for i in n_u]
        a_rk = [_bf(jnp.where(incl, abk[i][n2:, n2:], 0.0)) for i in n_u]
        first = ((si // 2) == (sj // 2)) & ((si % 2) == 1) & ((sj % 2) == 0)
        tinv = [eye + jnp.where(first, a_ab[i], 0.0) for i in n_u]
        s = 2
        while s < L:
            off = ((si // (2 * s)) == (sj // (2 * s))) & ((si % (2 * s)) >= s) & ((sj % (2 * s)) < s)
            t_b = [_bf(t) for t in tinv]
            x1 = [_dot(_bf(jnp.where(off, a_ab[i], 0.0)), t_b[i]) for i in n_u]
            x2 = [_dot(t_b[i], _bf(x1[i])) for i in n_u]
            tinv = [tinv[i] + x2[i] for i in n_u]
            s *= 2
        v2 = [jnp.concatenate([vv[i], vv[i]], axis=0) for i in n_u]
        av = [_dot(a_ak[i], v2[i]) for i in n_u]
        tx = [_dot(_bf(tinv[i]), jnp.concatenate([at_b[i], _bf(av[i])], axis=1)) for i in n_u]
        g1 = [_dot(a_rb[i], _bf(tx[i])) for i in n_u]
        g2 = [_dot(a_rk[i], v2[i]) for i in n_u]
        w = [_bf(merge_heads(tx[i][:, :LANES])) for i in n_u]
        uv = [jnp.concatenate([_bf(pick_heads(tx[i][:, LANES:])), vv[i]], axis=0) for i in n_u]
        phi = [_dot_tn(w[i], bl[i]) for i in n_u]
        psi = [_dot_tn(uv[i], jnp.concatenate([bl[i], kl[i]], axis=0)) for i in n_u]
        for i in n_u:
            c, j = units[i]
            rp_sc[bi, rws[i], cl[i]] = merge_heads(rt_s[i] + g1[i][:, :LANES])
            yl_sc[bi, rws[i], cl[i]] = pick_heads(g1[i][:, LANES:] + g2[i])
            phi_sc[bi, c, j] = jnp.where(same_head, phi[i], 0.0) + jnp.where(eye_mask, gl[i], 0.0)
            psi_sc[bi, c, j] = jnp.where(same_head, psi[i], 0.0)
        return bi

    for bi in range(n_b):
        lax.fori_loop(0, n_chunks // group, local_body, bi)

    def state_body(c, carry):
        rws = pl.ds(pl.multiple_of(c * L, L), L)
        chains = [(bi, j) for bi in range(n_b) for j in range(n_pairs)]
        n_c = range(len(chains))
        cl = [slice(j * LANES, (j + 1) * LANES) for _, j in chains]
        st = [_split(state_sc[bi, j]) for bi, j in chains]
        ph = [_split(phi_sc[bi, c, j]) for bi, j in chains]
        rp = [_bf(rp_sc[chains[i][0], rws, cl[i]]) for i in n_c]
        y0 = [_dot_nt(rp[i], st[i][0]) for i in n_c]
        n0 = [_dot(st[i][0], ph[i][0]) for i in n_c]
        n1 = [_dot(st[i][1], ph[i][0]) for i in n_c]
        n2_ = [_dot(st[i][0], ph[i][1]) for i in n_c]
        for i in n_c:
            bi, j = chains[i]
            rp_sc[bi, rws, cl[i]] = y0[i] + yl_sc[bi, rws, cl[i]]
            state_sc[bi, j] = n0[i] + n1[i] + n2_[i] + psi_sc[bi, c, j]
        return carry

    lax.fori_loop(0, n_chunks, state_body, 0)

    inv_n = 1.0 / HEAD_DIM
    for bi, j in [(bi, j) for bi in range(n_b) for j in range(n_pairs)]:
        cols = slice(j * LANES, (j + 1) * LANES)
        y = rp_sc[bi, :, cols]
        yc = y - _dot(_bf(y), head_sum) * inv_n
        var = _dot(_bf(yc * yc), head_sum) * inv_n
        yn = yc * lax.rsqrt(var + GN_EPS) * lng_ref[:, cols] + lnb_ref[:, cols]
        y_ref[bi, :, cols] = ((yn + bon_sc[bi, :, cols]) * og_ref[bi, :, cols].astype(F32)).astype(BF16)
    sout_ref[...] = state_sc[...]


def _rwkv(ops, state_in, prm, tm):
    bsz, t_len, _ = ops[0].shape
    n_pairs = RWKV_WIDTH // LANES
    n_chunks = tm // CHUNK
    full = lambda a: pl.BlockSpec(a.shape, lambda b, i: (0,) * a.ndim)
    n_b = 2 if bsz % 2 == 0 else 1
    row = pl.BlockSpec((n_b, tm, RWKV_WIDTH), lambda b, i: (b, i, 0))
    weights = [prm["k_k"], prm["k_a"], prm["r_k"], prm["ln_g"], prm["ln_b"]]
    tile = pltpu.VMEM((n_b, tm, RWKV_WIDTH), F32)
    mats = pltpu.VMEM((n_b, n_chunks, n_pairs, LANES, LANES), F32)
    return pl.pallas_call(
        functools.partial(_rwkv_kernel, tm=tm),
        grid=(bsz // n_b, t_len // tm),
        in_specs=[row] * len(ops) + [full(state_in)] + [full(a) for a in weights],
        out_specs=[row, pl.BlockSpec((n_b, n_pairs, LANES, LANES), lambda b, i: (b, 0, 0, 0))],
        out_shape=[jax.ShapeDtypeStruct((bsz, t_len, RWKV_WIDTH), BF16),
                   jax.ShapeDtypeStruct((bsz, n_pairs, LANES, LANES), F32)],
        scratch_shapes=[pltpu.VMEM((n_b, n_pairs, LANES, LANES), F32)] + [tile] * 6
                       + [pltpu.VMEM((n_b, n_chunks * 8, RWKV_WIDTH), F32)] + [tile] * 3 + [mats] * 2,
        compiler_params=_params("parallel", "arbitrary"),
        name="rwkv",
    )(*ops, state_in, *weights)


def _s5_kernel(u_ref, sin_ref, m_ref, q_ref, p_ref, lr_ref, li_ref, d_ref,
               y_ref, sout_ref, st_sc, *, rows):
    half = S5_BLOCK_STATE

    @pl.when(pl.program_id(2) == 0)
    def _():
        st_sc[...] = sin_ref[...]

    u = jnp.concatenate([u_ref[pl.ds(t, rows, stride=S5_TOK), :] for t in range(S5_TOK)], axis=1)
    ub = u.astype(BF16)
    z = _dot(ub, q_ref[...])
    y_in = _dot(ub, m_ref[...]) + d_ref[...] * u

    row = lax.broadcasted_iota(jnp.int32, (rows, half), 0)
    sub = row % 8
    xr = z[:, :half]
    xi = z[:, half:]
    s = 1
    while s < min(8, rows):
        ar = lr_ref[s - 1:s, :]
        ai = li_ref[s - 1:s, :]
        pr = jnp.where(sub >= s, pltpu.roll(xr, s, 0), 0.0)
        pi = jnp.where(sub >= s, pltpu.roll(xi, s, 0), 0.0)
        xr, xi = xr + (ar * pr - ai * pi), xi + (ar * pi + ai * pr)
        s *= 2
    tr = lr_ref[...]
    ti = li_ref[...]
    cr = st_sc[0:1, :]
    ci = st_sc[1:2, :]
    out_r, out_i = [], []
    for blk in range(rows // 8):
        br = xr[blk * 8:(blk + 1) * 8] + (tr * cr - ti * ci)
        bi = xi[blk * 8:(blk + 1) * 8] + (tr * ci + ti * cr)
        out_r.append(br)
        out_i.append(bi)
        cr = br[7:8]
        ci = bi[7:8]
    xr = jnp.concatenate(out_r, axis=0)
    xi = jnp.concatenate(out_i, axis=0)
    sr = st_sc[0:1, :]
    si = st_sc[1:2, :]
    st_sc[0:1, :] = cr
    st_sc[1:2, :] = ci
    sout_ref[...] = st_sc[...]
    x0 = jnp.concatenate([jnp.where(row == 0, sr, pltpu.roll(xr, 1, 0)),
                          jnp.where(row == 0, si, pltpu.roll(xi, 1, 0))], axis=1)
    ya = _gelu_tanh(y_in + _dot(_bf(x0), p_ref[...]))
    for t in range(S5_TOK):
        y_ref[pl.ds(t, rows, stride=S5_TOK), :] = ya[:, t * LANES:(t + 1) * LANES]


def _s5(u, state_in, tab, rows):
    n_blk, bsz, t_len, _ = u.shape
    tm = rows * S5_TOK
    blk = lambda a: pl.BlockSpec((None,) + a.shape[1:], lambda b, k, i: (k,) + (0,) * (a.ndim - 1))
    tokens = pl.BlockSpec((None, None, tm, LANES), lambda b, k, i: (k, b, i, 0))
    weights = [tab["m"], tab["q"], tab["p"], tab["lr"], tab["li"], tab["d"]]
    return pl.pallas_call(
        functools.partial(_s5_kernel, rows=rows),
        grid=(bsz, n_blk, t_len // tm),
        in_specs=[tokens, blk(state_in)] + [blk(a) for a in weights],
        out_specs=[tokens, pl.BlockSpec((None, None, 2, S5_BLOCK_STATE), lambda b, k, i: (b, k, 0, 0))],
        out_shape=[jax.ShapeDtypeStruct(u.shape, F32),
                   jax.ShapeDtypeStruct((bsz, n_blk, 2, S5_BLOCK_STATE), F32)],
        scratch_shapes=[pltpu.VMEM((2, S5_BLOCK_STATE), F32)],
        compiler_params=_params("parallel", "arbitrary", "arbitrary"),
        name="s5",
    )(u, state_in, *weights)


def _s5_tables(lam_re, lam_im, log_dt, b_re, b_im, c_re, c_im, d):
    f = lambda a: a.astype(F32)
    re = jnp.minimum(f(lam_re), -1e-4)
    im = f(lam_im)
    dt = jnp.exp(f(log_dt))[:, None]

    def power(m):
        mag = jnp.exp(m * (re * dt))
        ang = m * (im * dt)
        return mag * jnp.cos(ang), mag * jnp.sin(ang)

    l1r, l1i = power(1.0)
    den = re * re + im * im
    cr = ((l1r - 1.0) * re + l1i * im) / den
    ci = (l1i * re - (l1r - 1.0) * im) / den
    bbr = cr[..., None] * f(b_re) - ci[..., None] * f(b_im)
    bbi = cr[..., None] * f(b_im) + ci[..., None] * f(b_re)
    pw = [power(float(m)) for m in range(S5_TOK + 1)]
    pwr = jnp.stack([x[0] for x in pw])
    pwi = jnp.stack([x[1] for x in pw])
    ccr, cci = f(c_re), f(c_im)
    n_blk = S5_WIDTH // LANES
    gpb = LANES // S5_GROUP
    big = S5_TOK * LANES

    def widen(src, row_grp, col_grp, sub):
        ci = lax.broadcasted_iota(jnp.int32, (LANES, big), 0)
        cj = lax.broadcasted_iota(jnp.int32, (LANES, big), 1)
        spread = ((ci // sub == cj // (sub * gpb)) & (ci % sub == cj % sub)).astype(BF16)
        ri = lax.broadcasted_iota(jnp.int32, (big, big), 0)
        rj = lax.broadcasted_iota(jnp.int32, (big, big), 1)
        keep = ((ri // row_grp) % gpb == (rj // col_grp) % gpb).astype(F32)
        wide = jnp.einsum("brk,kc->brc", src.astype(BF16), spread, preferred_element_type=F32)
        return (wide * keep[None]).astype(BF16)

    pbr = pwr[:S5_TOK, :, :, None] * bbr[None] - pwi[:S5_TOK, :, :, None] * bbi[None]
    pbi = pwr[:S5_TOK, :, :, None] * bbi[None] + pwi[:S5_TOK, :, :, None] * bbr[None]
    kern = (jnp.einsum("gon,mgnh->mgoh", ccr, pbr, precision=HI)
            - jnp.einsum("gon,mgnh->mgoh", cci, pbi, precision=HI))
    ti = jnp.arange(S5_TOK)[:, None]
    to = jnp.arange(S5_TOK)[None, :]
    diff = to - ti
    toep = jnp.where((diff >= 0)[:, :, None, None, None], kern[jnp.clip(diff, 0, S5_TOK - 1)], 0.0)
    m6 = toep.reshape(S5_TOK, S5_TOK, n_blk, gpb, S5_GROUP, S5_GROUP).transpose(2, 0, 3, 5, 1, 4)
    m_big = widen(m6.reshape(n_blk, big, LANES), S5_GROUP, S5_GROUP, S5_GROUP)

    q2 = jnp.stack([pbr[::-1], pbi[::-1]])
    q6 = q2.reshape(2, S5_TOK, n_blk, gpb, S5_STATE, S5_GROUP).transpose(2, 1, 3, 5, 0, 4)
    q_big = widen(q6.reshape(n_blk, big, LANES), S5_GROUP, S5_STATE, S5_STATE)
    cpr = ccr[None] * pwr[1:, :, None, :] - cci[None] * pwi[1:, :, None, :]
    cpi = ccr[None] * pwi[1:, :, None, :] + cci[None] * pwr[1:, :, None, :]
    p2 = jnp.stack([cpr, -cpi])
    p6 = p2.reshape(2, S5_TOK, n_blk, gpb, S5_GROUP, S5_STATE).transpose(2, 0, 3, 5, 1, 4)
    p_big = widen(p6.reshape(n_blk, big, LANES), S5_STATE, S5_GROUP, S5_GROUP)

    lv = [power(float(S5_TOK * (rr + 1))) for rr in range(8)]
    lr = jnp.stack([x[0].reshape(n_blk, -1) for x in lv], axis=1)
    li = jnp.stack([x[1].reshape(n_blk, -1) for x in lv], axis=1)
    d_cols = jnp.broadcast_to(f(d).reshape(n_blk, 1, 1, LANES), (n_blk, 1, S5_TOK, LANES)).reshape(n_blk, 1, -1)
    return {"m": m_big, "q": q_big, "p": p_big, "lr": lr, "li": li, "d": d_cols}


def _mix_kernel(h_ref, gate_ref, ya_ref, yb_ref, wbr_ref, wglu_ref, wout_ref, gpost_ref, gpre_ref,
                h1_ref, hn_ref, *, tm):
    n_sub = 4 if tm % 1024 == 0 else 1
    sub = tm // n_sub
    rws = [slice(q * sub, (q + 1) * sub) for q in range(n_sub)]
    y_a = [_dot(ya_ref[r, :], wbr_ref[...]) for r in rws]
    yb = [jnp.concatenate([yb_ref[blk, r, :] for blk in range(S5_WIDTH // LANES)], axis=1).astype(BF16)
          for r in rws]
    z = [_dot(x, wglu_ref[...]) for x in yb]
    mix = []
    for q, r in enumerate(rws):
        y_b = z[q][:, :D_MODEL] * jax.nn.sigmoid(z[q][:, D_MODEL:])
        gates = gate_ref[r, :].astype(F32)
        mix.append((gates[:, :D_MODEL] * y_a[q] + gates[:, D_MODEL:] * y_b).astype(BF16))
    o = [_dot(m, wout_ref[...]) for m in mix]
    for q, r in enumerate(rws):
        h1 = h_ref[r, :] + _rms(o[q], gpost_ref[...])
        h1_ref[r, :] = h1
        hn_ref[r, :] = _rms(h1, gpre_ref[...]).astype(BF16)


def _mix(h, gates, ya, yb, prm, tm):
    bsz, t_len, _ = h.shape
    row = lambda c: pl.BlockSpec((None, tm, c), lambda b, i: (b, i, 0))
    full = lambda a: pl.BlockSpec(a.shape, lambda b, i: (0,) * a.ndim)
    weights = [prm["w_branch"], prm["w_glu"], prm["w_out"], prm["g_post"], prm["g_ffn_pre"]]
    return pl.pallas_call(
        functools.partial(_mix_kernel, tm=tm),
        grid=(bsz, t_len // tm),
        in_specs=[row(D_MODEL), row(2 * D_MODEL), row(RWKV_WIDTH), _lane_blocks(tm)] + [_resident(a) for a in weights],
        out_specs=[row(D_MODEL), row(D_MODEL)],
        out_shape=[jax.ShapeDtypeStruct((bsz, t_len, D_MODEL), F32),
                   jax.ShapeDtypeStruct((bsz, t_len, D_MODEL), BF16)],
        compiler_params=_params("parallel", "parallel"),
        name="mix",
    )(h, gates, ya, yb, *weights)


def _ffn_kernel(hn_ref, h1_ref, halo_ref, wup_ref, cw_ref, cb_ref, wd_ref, gpost_ref,
                out_ref, halo_out_ref, halo_sc, *, tm, tf):
    @pl.when(pl.program_id(1) == 0)
    def _():
        halo_sc[...] = halo_ref[...]

    n_sub = 2 if tm % 512 == 0 else 1
    sub = tm // n_sub
    rws = [slice(q * sub, (q + 1) * sub) for q in range(n_sub)]
    row = lax.broadcasted_iota(jnp.int32, (sub, tf), 0)
    acc = [None] * n_sub
    for f in range(D_FF // tf):
        cols = slice(f * tf, (f + 1) * tf)
        hn = [hn_ref[r, :] for r in rws]
        ua = [_dot(x, wup_ref[:, cols]) for x in hn]
        ub = [_dot(x, wup_ref[:, D_FF + f * tf:D_FF + (f + 1) * tf]) for x in hn]
        tails = [halo_sc[:, cols]] + [x[sub - 8:, :] for x in ua]
        act = []
        for q in range(n_sub):
            prev1 = tails[q][7:8, :]
            prev2 = tails[q][6:7, :]
            u1 = jnp.where(row == 0, prev1, pltpu.roll(ua[q], 1, 0))
            u2 = jnp.where(row == 0, prev2, jnp.where(row == 1, prev1, pltpu.roll(ua[q], 2, 0)))
            conv = cw_ref[0:1, cols] * u2 + cw_ref[1:2, cols] * u1 + cw_ref[2:3, cols] * ua[q] + cb_ref[:, cols]
            act.append((_gelu_tanh(conv) * ub[q]).astype(BF16))
        halo_sc[:, cols] = tails[n_sub]
        part = [_dot(x, wd_ref[cols, :]) for x in act]
        acc = [part[q] if acc[q] is None else acc[q] + part[q] for q in range(n_sub)]
    halo_out_ref[...] = halo_sc[...]
    for q, r in enumerate(rws):
        out_ref[r, :] = h1_ref[r, :] + _rms(acc[q], gpost_ref[...])


def _ffn(hn, h1, halo_in, prm, tm, tf):
    bsz, t_len, _ = hn.shape
    n_t = t_len // tm
    row = lambda c: pl.BlockSpec((None, tm, c), lambda b, i: (b, i, 0))
    weights = [prm["w_up"], prm["conv_w"], prm["conv_b"], prm["w_down"], prm["g_ffn_post"]]
    return pl.pallas_call(
        functools.partial(_ffn_kernel, tm=tm, tf=tf),
        grid=(bsz, n_t),
        in_specs=[row(D_MODEL), row(D_MODEL), _resident(halo_in)] + [_resident(a) for a in weights],
        out_specs=[row(D_MODEL), pl.BlockSpec((None, None, 8, D_FF), lambda b, i: (b, i, 0, 0))],
        out_shape=[jax.ShapeDtypeStruct((bsz, t_len, D_MODEL), F32),
                   jax.ShapeDtypeStruct((bsz, n_t, 8, D_FF), F32)],
        scratch_shapes=[pltpu.VMEM((8, D_FF), F32)],
        compiler_params=_params("parallel", "arbitrary"),
        name="ffn",
    )(hn, h1, halo_in, *weights)


def _pick(t_len, options):
    for o in options:
        if t_len % o == 0:
            return o
    raise ValueError(f"no tile for sequence length {t_len}")


def _block(h, carry, prm, tab, n_pad=0):
    t_len = h.shape[1]
    keep = (jnp.arange(t_len) >= n_pad)[None, :, None]
    tm_proj = _pick(t_len, (512, 64))
    tm_rwkv = _pick(t_len, (256, 64))
    rows_s5 = _pick(t_len // S5_TOK, (512, 8))
    tm_mix = _pick(t_len, (1024, 64))
    tm_ffn = _pick(t_len, (512, 64))
    *rwkv_ops, u, gates, last_rows = _inproj(h, carry["prev_row"], prm, tm_proj)
    ya, rwkv_state = _rwkv(rwkv_ops, carry["rwkv_state"], prm, tm_rwkv)
    yb, s5_state = _s5(u, carry["s5_state"], tab, rows_s5)
    h1, hn2 = _mix(h, gates, ya, yb, prm, tm_mix)
    if n_pad:
        hn2 = jnp.where(keep, hn2, jnp.zeros_like(hn2))
    out, halo = _ffn(hn2, h1, carry["halo"], prm, tm_ffn, D_FF)
    if n_pad:
        out = jnp.where(keep, out, jnp.zeros_like(out))
    new_carry = {"prev_row": last_rows[0, -1], "rwkv_state": rwkv_state[0], "s5_state": s5_state[0],
                 "halo": halo[0, -1]}
    return out, new_carry


def kernel(x, meta_tokens, norm_mix_pre, norm_mix_post, w_in, rwkv_shift_mu, rwkv_w0, rwkv_w_up, rwkv_a0, rwkv_a_up, rwkv_g_up, rwkv_k_k, rwkv_k_a, rwkv_r_k, rwkv_ln_gain, rwkv_ln_bias, rwkv_w_branch, s5_lambda_re, s5_lambda_im, s5_log_dt, s5_b_re, s5_b_im, s5_c_re, s5_c_im, s5_d, s5_w_glu, w_out, norm_ffn_pre, norm_ffn_post, ffn_w_up, ffn_conv_w, ffn_conv_b, ffn_w_down):
    depth = w_in.shape[0]
    n_pairs = RWKV_WIDTH // LANES
    zeros_lora = jnp.zeros((W_LORA, RWKV_WIDTH), F32)

    h_meta = jnp.concatenate([jnp.zeros((META_PAD - N_META, D_MODEL), x.dtype), meta_tokens.astype(x.dtype)])[None]
    h_main = x
    for l in range(depth):
        vec = lambda a: a[l].reshape(1, -1).astype(F32)
        prm = {
            "g_pre": vec(norm_mix_pre), "g_post": vec(norm_mix_post),
            "g_ffn_pre": vec(norm_ffn_pre), "g_ffn_post": vec(norm_ffn_post),
            "w_in": w_in[l].astype(BF16),
            "mu": vec(rwkv_shift_mu), "w0": vec(rwkv_w0), "a0": vec(rwkv_a0),
            "wup": jnp.concatenate([rwkv_w_up[l].astype(F32), zeros_lora]),
            "aup": jnp.concatenate([zeros_lora, rwkv_a_up[l].astype(F32)]),
            "gup": rwkv_g_up[l].astype(F32),
            "k_k": vec(rwkv_k_k), "k_a": vec(rwkv_k_a), "r_k": vec(rwkv_r_k),
            "ln_g": vec(rwkv_ln_gain), "ln_b": vec(rwkv_ln_bias),
            "w_branch": rwkv_w_branch[l].astype(BF16), "w_glu": s5_w_glu[l].astype(BF16),
            "w_out": w_out[l].astype(BF16), "w_up": ffn_w_up[l].astype(BF16),
            "conv_w": ffn_conv_w[l].astype(F32), "conv_b": vec(ffn_conv_b),
            "w_down": ffn_w_down[l].astype(BF16),
        }
        tab = _s5_tables(s5_lambda_re[l], s5_lambda_im[l], s5_log_dt[l], s5_b_re[l],
                         s5_b_im[l], s5_c_re[l], s5_c_im[l], s5_d[l])
        zero_carry = {"prev_row": jnp.zeros((1, RWKV_PROJ), F32),
                      "rwkv_state": jnp.zeros((n_pairs, LANES, LANES), F32),
                      "s5_state": jnp.zeros((S5_WIDTH // LANES, 2, S5_BLOCK_STATE), F32),
                      "halo": jnp.zeros((8, D_FF), F32)}
        h_meta, carry = _block(h_meta, zero_carry, prm, tab, n_pad=META_PAD - N_META)
        h_main, _ = _block(h_main, carry, prm, tab)
    return h_main
```

```python
import functools
import math

import jax
import jax.numpy as jnp
from jax import lax
from jax.experimental import pallas as pl
from jax.experimental.pallas import tpu as pltpu

F32 = jnp.float32
BF16 = jnp.bfloat16
HI = lax.Precision.HIGHEST

D_MODEL = 1024
N_META = 16
RWKV_WIDTH = 512
HEAD_DIM = 64
W_LORA = 64
A_LORA = 64
G_LORA = 128
RWKV_PROJ = 3 * RWKV_WIDTH + W_LORA + A_LORA + G_LORA
GN_EPS = 64e-5
S5_WIDTH = 512
S5_GROUP = 16
S5_STATE = 64
D_FF = 2816
RMS_EPS = 1e-6

LANES = 128
CHUNK = 64
CHUNK_GROUP = 4
S5_TOK = 8
S5_BLOCK_STATE = (LANES // S5_GROUP) * S5_STATE
META_PAD = 64
VMEM_LIMIT = 56 * 1024 * 1024


def _dot(a, b, precision=None):
    return jnp.dot(a, b, preferred_element_type=F32, precision=precision)


def _dot_nt(a, b):
    return lax.dot_general(a, b, (((1,), (1,)), ((), ())), preferred_element_type=F32)


def _dot_tn(a, b):
    return lax.dot_general(a, b, (((0,), (0,)), ((), ())), preferred_element_type=F32)


def _bf(x):
    return x.astype(BF16)


def _split(x):
    hi = x.astype(BF16)
    return hi, (x - hi.astype(F32)).astype(BF16)


def _dot_x3(a, b):
    ah, al = _split(a)
    bh, bl = _split(b)
    return _dot(ah, bh) + _dot(al, bh) + _dot(ah, bl)


def _gelu_tanh(x):
    c = math.sqrt(2.0 / math.pi)
    return 0.5 * x * (1.0 + jnp.tanh(c * (x + 0.044715 * (x * x * x))))


def _softplus(x):
    return jnp.maximum(x, 0.0) + jnp.log(1.0 + jnp.exp(-jnp.abs(x)))


def _rms(x, g):
    ms = jnp.mean(x * x, axis=-1, keepdims=True)
    return x * lax.rsqrt(ms + RMS_EPS) * g


def _params(*sem):
    return pltpu.CompilerParams(dimension_semantics=sem, vmem_limit_bytes=VMEM_LIMIT)


def _resident(a):
    return pl.BlockSpec(a.shape, lambda b, i: (0,) * a.ndim, pipeline_mode=pl.Buffered(1))


def _lane_blocks(tm):
    return pl.BlockSpec((S5_WIDTH // LANES, None, tm, LANES), lambda b, i: (0, b, i, 0))


def _inproj_kernel(h_ref, prev_ref, g_ref, w_ref, mu_ref, w0_ref, wup_ref, a0_ref, aup_ref, gup_ref,
                   logw_ref, r_ref, k_ref, v_ref, a_ref, og_ref, u_ref, gate_ref, last_ref,
                   prev_sc, *, tm):
    @pl.when(pl.program_id(1) == 0)
    def _():
        prev_sc[...] = prev_ref[...]

    n_sub = 2 if tm % 512 == 0 else 1
    sub = tm // n_sub
    rws = [slice(q * sub, (q + 1) * sub) for q in range(n_sub)]
    subs = range(n_sub)
    hn = [_rms(h_ref[r, :], g_ref[...]).astype(BF16) for r in rws]
    n_gate = gate_ref.shape[-1]
    gate_cols = n_gate // 4

    def gate_part(q):
        lo = RWKV_PROJ + S5_WIDTH + q * gate_cols
        for i in subs:
            gate_ref[rws[i], q * gate_cols:(q + 1) * gate_cols] = jax.nn.sigmoid(
                _dot(hn[i], w_ref[:, lo:lo + gate_cols])).astype(BF16)

    p = [_dot(x, w_ref[:, 0:RWKV_PROJ]) for x in hn]
    row = lax.broadcasted_iota(jnp.int32, p[0].shape, 0)
    lasts = [prev_sc[...]] + [x[sub - 1:, :] for x in p]
    prev_sc[...] = lasts[n_sub]
    last_ref[...] = lasts[n_sub]
    pm = []
    for i in subs:
        ps = jnp.where(row == 0, lasts[i], pltpu.roll(p[i], 1, 0))
        pm.append(p[i] + (ps - p[i]) * mu_ref[...])
    for i in subs:
        r_ref[rws[i], :] = pm[i][:, 0:RWKV_WIDTH].astype(BF16)
        k_ref[rws[i], :] = pm[i][:, RWKV_WIDTH:2 * RWKV_WIDTH].astype(BF16)
        v_ref[rws[i], :] = pm[i][:, 2 * RWKV_WIDTH:3 * RWKV_WIDTH].astype(BF16)
    gate_part(0)
    wa = [x[:, 3 * RWKV_WIDTH:3 * RWKV_WIDTH + W_LORA + A_LORA] for x in pm]
    gd = [x[:, 3 * RWKV_WIDTH + W_LORA + A_LORA:] for x in pm]
    for i in subs:
        w_log = -_softplus(-(w0_ref[...] + _dot_x3(jnp.tanh(wa[i]), wup_ref[...]))) - 0.5
        logw_ref[rws[i], :] = -jnp.exp(w_log)
    gate_part(1)
    for i in subs:
        a_ref[rws[i], :] = jax.nn.sigmoid(a0_ref[...] + _dot_x3(wa[i], aup_ref[...])).astype(BF16)
        og_ref[rws[i], :] = _dot(_bf(jax.nn.sigmoid(gd[i])), _bf(gup_ref[...])).astype(BF16)
    gate_part(2)
    gate_part(3)
    for i in subs:
        u = _dot(hn[i], w_ref[:, RWKV_PROJ:RWKV_PROJ + S5_WIDTH])
        for blk in range(S5_WIDTH // LANES):
            u_ref[blk, rws[i], :] = u[:, blk * LANES:(blk + 1) * LANES]


def _inproj(h, prev_row, prm, tm):
    bsz, t_len, _ = h.shape
    n_t = t_len // tm
    w = prm["w_in"]
    n_gate = w.shape[1] - RWKV_PROJ - S5_WIDTH
    row = lambda c: pl.BlockSpec((None, tm, c), lambda b, i: (b, i, 0))
    weights = [prm["g_pre"], w, prm["mu"], prm["w0"], prm["wup"], prm["a0"], prm["aup"], prm["gup"]]
    act = lambda dt: jax.ShapeDtypeStruct((bsz, t_len, RWKV_WIDTH), dt)
    return pl.pallas_call(
        functools.partial(_inproj_kernel, tm=tm),
        grid=(bsz, n_t),
        in_specs=[row(D_MODEL), _resident(prev_row)] + [_resident(a) for a in weights],
        out_specs=[row(RWKV_WIDTH)] * 6 + [_lane_blocks(tm), row(n_gate),
                                           pl.BlockSpec((None, None, 1, RWKV_PROJ), lambda b, i: (b, i, 0, 0))],
        out_shape=[act(F32)] + [act(BF16)] * 5
                  + [jax.ShapeDtypeStruct((S5_WIDTH // LANES, bsz, t_len, LANES), F32),
                     jax.ShapeDtypeStruct((bsz, t_len, n_gate), BF16),
                     jax.ShapeDtypeStruct((bsz, n_t, 1, RWKV_PROJ), F32)],
        scratch_shapes=[pltpu.VMEM((1, RWKV_PROJ), F32)],
        compiler_params=_params("parallel", "arbitrary"),
        name="inproj",
    )(h, prev_row, *weights)


def _rwkv_kernel(logw_ref, r_ref, k_ref, v_ref, a_ref, og_ref, sin_ref, kk_ref, ka_ref, rk_ref, lng_ref, lnb_ref,
                 y_ref, sout_ref,
                 state_sc, at_sc, rt_sc, bh_sc, kh_sc, bl_sc, kl_sc, gl_sc, bon_sc, rp_sc, yl_sc, phi_sc, psi_sc, *, tm):
    n_b = y_ref.shape[0]
    L = CHUNK
    n_chunks = tm // L
    n_pairs = RWKV_WIDTH // LANES
    n2 = 2 * L

    @pl.when(pl.program_id(1) == 0)
    def _():
        for bi in range(n_b):
            state_sc[bi] = sin_ref[...]

    si = lax.broadcasted_iota(jnp.int32, (n2, n2), 0)
    sj = lax.broadcasted_iota(jnp.int32, (n2, n2), 1)
    same_head = (si // L) == (sj // L)
    strict = same_head & ((si % L) > (sj % L))
    incl = same_head & ((si % L) >= (sj % L))
    eye_mask = si == sj
    eye = eye_mask.astype(F32)
    head_sum = same_head.astype(BF16)
    head0 = lax.broadcasted_iota(jnp.int32, (L, LANES), 1) < HEAD_DIM

    li = lax.broadcasted_iota(jnp.int32, (L, L), 0)
    lj = lax.broadcasted_iota(jnp.int32, (L, L), 1)
    ltri = (li >= lj).astype(BF16)
    for bi, j in [(bi, j) for bi in range(n_b) for j in range(n_pairs)]:
        cols = slice(j * LANES, (j + 1) * LANES)
        k_j = k_ref[bi, :, cols].astype(F32)
        a_j = a_ref[bi, :, cols].astype(F32)
        r_j = r_ref[bi, :, cols].astype(F32)
        kx = k_j * kk_ref[:, cols]
        kkn_j = kx * lax.rsqrt(jnp.maximum(_dot(_bf(kx * kx), head_sum), 1e-24))
        scale = 1.0 + (a_j - 1.0) * ka_ref[:, cols]
        kp_j = k_j * scale
        b_j = kkn_j * a_j
        bon_sc[bi, :, cols] = (_dot(_bf(r_j * kp_j * rk_ref[:, cols]), head_sum)
                               * v_ref[bi, :, cols].astype(F32))
        for c in range(n_chunks):
            rows = slice(c * L, (c + 1) * L)
            lw = logw_ref[bi, rows, cols]
            lw_hi, lw_lo = _split(lw)
            cum = _dot(ltri, lw_hi) + _dot(ltri, lw_lo)
            cum_l = jnp.sum(lw, axis=0, keepdims=True)
            b = b_j[rows]
            kp = kp_j[rows]
            at_sc[bi, rows, cols] = -kkn_j[rows] * jnp.exp(cum - lw)
            rt_sc[bi, rows, cols] = r_j[rows] * jnp.exp(cum)
            ginv = jnp.exp(-cum)
            bh_sc[bi, rows, cols] = b * ginv
            kh_sc[bi, rows, cols] = kp * ginv
            g_l = jnp.exp(cum_l)
            tail = g_l * ginv
            bl_sc[bi, rows, cols] = b * tail
            kl_sc[bi, rows, cols] = kp * tail
            gl_sc[bi, c * 8:(c + 1) * 8, cols] = jnp.broadcast_to(g_l, (8, LANES))

    def split_heads(x):
        return jnp.concatenate([jnp.where(head0, x, 0.0), jnp.where(head0, 0.0, x)], axis=0)

    def pick_heads(xs):
        return jnp.where(head0, xs[:L], xs[L:])

    def merge_heads(xs):
        return xs[:L] + xs[L:]

    group = min(CHUNK_GROUP, n_chunks)

    def local_body(cg, bi):
        units = [(cg * group + dc, j) for dc in range(group) for j in range(n_pairs)]
        n_u = range(len(units))
        rws = [pl.ds(pl.multiple_of(c * L, L), L) for c, _ in units]
        cl = [slice(j * LANES, (j + 1) * LANES) for _, j in units]
        at_s = [split_heads(at_sc[bi, rws[i], cl[i]]) for i in n_u]
        rt_s = [split_heads(rt_sc[bi, rws[i], cl[i]]) for i in n_u]
        bl = [_bf(bl_sc[bi, rws[i], cl[i]]) for i in n_u]
        kl = [_bf(kl_sc[bi, rws[i], cl[i]]) for i in n_u]
        vv = [v_ref[bi, rws[i], cl[i]] for i in n_u]
        gl = [gl_sc[bi, pl.ds(pl.multiple_of(units[i][0] * 8, 8), 1), cl[i]] for i in n_u]
        bk_s = [_bf(jnp.concatenate([split_heads(bh_sc[bi, rws[i], cl[i]]), split_heads(kh_sc[bi, rws[i], cl[i]])],
                                    axis=0)) for i in n_u]
        at_b = [_bf(x) for x in at_s]
        lhs = [jnp.concatenate([at_b[i], _bf(rt_s[i])], axis=0) for i in n_u]
        abk = [_dot_nt(lhs[i], bk_s[i]) for i in n_u]
        a_ab = [jnp.where(strict, abk[i][:n2, :n2], 0.0) for i in n_u]
        a_ak = [_bf(jnp.where(strict, abk[i][:n2, n2:], 0.0)) for i in n_u]
        a_rb = [_bf(jnp.where(incl, abk[i][n2:, :n2], 0.0)) for i in n_u]
        a_rk = [_bf(jnp.where(incl, abk[i][n2:, n2:], 0.0)) for i in n_u]
        first = ((si // 2) == (sj // 2)) & ((si % 2) == 1) & ((sj % 2) == 0)
        tinv = [eye + jnp.where(first, a_ab[i], 0.0) for i in n_u]
        s = 2
        while s < L:
            off = ((si // (2 * s)) == (sj // (2 * s))) & ((si % (2 * s)) >= s) & ((sj % (2 * s)) < s)
            t_b = [_bf(t) for t in tinv]
            x1 = [_dot(_bf(jnp.where(off, a_ab[i], 0.0)), t_b[i]) for i in n_u]
            x2 = [_dot(t_b[i], _bf(x1[i])) for i in n_u]
            tinv = [tinv[i] + x2[i] for i in n_u]
            s *= 2
        v2 = [jnp.concatenate([vv[i], vv[i]], axis=0) for i in n_u]
        av = [_dot(a_ak[i], v2[i]) for i in n_u]
        tx = [_dot(_bf(tinv[i]), jnp.concatenate([at_b[i], _bf(av[i])], axis=1)) for i in n_u]
        g1 = [_dot(a_rb[i], _bf(tx[i])) for i in n_u]
        g2 = [_dot(a_rk[i], v2[i]) for i in n_u]
        w = [_bf(merge_heads(tx[i][:, :LANES])) for i in n_u]
        uv = [jnp.concatenate([_bf(pick_heads(tx[i][:, LANES:])), vv[i]], axis=0) for i in n_u]
        phi = [_dot_tn(w[i], bl[i]) for i in n_u]
        psi = [_dot_tn(uv[i], jnp.concatenate([bl[i], kl[i]], axis=0)) for i in n_u]
        for i in n_u:
            c, j = units[i]
            rp_sc[bi, rws[i], cl[i]] = merge_heads(rt_s[i] + g1[i][:, :LANES])
            yl_sc[bi, rws[i], cl[i]] = pick_heads(g1[i][:, LANES:] + g2[i])
            phi_sc[bi, c, j] = jnp.where(same_head, phi[i], 0.0) + jnp.where(eye_mask, gl[i], 0.0)
            psi_sc[bi, c, j] = jnp.where(same_head, psi[i], 0.0)
        return bi

    for bi in range(n_b):
        lax.fori_loop(0, n_chunks // group, local_body, bi)

    def state_body(c, carry):
        rws = pl.ds(pl.multiple_of(c * L, L), L)
        chains = [(bi, j) for bi in range(n_b) for j in range(n_pairs)]
        n_c = range(len(chains))
        cl = [slice(j * LANES, (j + 1) * LANES) for _, j in chains]
        st = [_split(state_sc[bi, j]) for bi, j in chains]
        ph = [_split(phi_sc[bi, c, j]) for bi, j in chains]
        rp = [_bf(rp_sc[chains[i][0], rws, cl[i]]) for i in n_c]
        y0 = [_dot_nt(rp[i], st[i][0]) for i in n_c]
        n0 = [_dot(st[i][0], ph[i][0]) for i in n_c]
        n1 = [_dot(st[i][1], ph[i][0]) for i in n_c]
        n2_ = [_dot(st[i][0], ph[i][1]) for i in n_c]
        for i in n_c:
            bi, j = chains[i]
            rp_sc[bi, rws, cl[i]] = y0[i] + yl_sc[bi, rws, cl[i]]
            state_sc[bi, j] = n0[i] + n1[i] + n2_[i] + psi_sc[bi, c, j]
        return carry

    lax.fori_loop(0, n_chunks, state_body, 0)

    inv_n = 1.0 / HEAD_DIM
    for bi, j in [(bi, j) for bi in range(n_b) for j in range(n_pairs)]:
        cols = slice(j * LANES, (j + 1) * LANES)
        y = rp_sc[bi, :, cols]
        yc = y - _dot(_bf(y), head_sum) * inv_n
        var = _dot(_bf(yc * yc), head_sum) * inv_n
        yn = yc * lax.rsqrt(var + GN_EPS) * lng_ref[:, cols] + lnb_ref[:, cols]
        y_ref[bi, :, cols] = ((yn + bon_sc[bi, :, cols]) * og_ref[bi, :, cols].astype(F32)).astype(BF16)
    sout_ref[...] = state_sc[...]


def _rwkv(ops, state_in, prm, tm):
    bsz, t_len, _ = ops[0].shape
    n_pairs = RWKV_WIDTH // LANES
    n_chunks = tm // CHUNK
    full = lambda a: pl.BlockSpec(a.shape, lambda b, i: (0,) * a.ndim)
    n_b = 2 if bsz % 2 == 0 else 1
    row = pl.BlockSpec((n_b, tm, RWKV_WIDTH), lambda b, i: (b, i, 0))
    weights = [prm["k_k"], prm["k_a"], prm["r_k"], prm["ln_g"], prm["ln_b"]]
    tile = pltpu.VMEM((n_b, tm, RWKV_WIDTH), F32)
    mats = pltpu.VMEM((n_b, n_chunks, n_pairs, LANES, LANES), F32)
    return pl.pallas_call(
        functools.partial(_rwkv_kernel, tm=tm),
        grid=(bsz // n_b, t_len // tm),
        in_specs=[row] * len(ops) + [full(state_in)] + [full(a) for a in weights],
        out_specs=[row, pl.BlockSpec((n_b, n_pairs, LANES, LANES), lambda b, i: (b, 0, 0, 0))],
        out_shape=[jax.ShapeDtypeStruct((bsz, t_len, RWKV_WIDTH), BF16),
                   jax.ShapeDtypeStruct((bsz, n_pairs, LANES, LANES), F32)],
        scratch_shapes=[pltpu.VMEM((n_b, n_pairs, LANES, LANES), F32)] + [tile] * 6
                       + [pltpu.VMEM((n_b, n_chunks * 8, RWKV_WIDTH), F32)] + [tile] * 3 + [mats] * 2,
        compiler_params=_params("parallel", "arbitrary"),
        name="rwkv",
    )(*ops, state_in, *weights)


def _s5_kernel(u_ref, sin_ref, m_ref, q_ref, p_ref, lr_ref, li_ref, d_ref,
               y_ref, sout_ref, st_sc, *, rows):
    half = S5_BLOCK_STATE

    @pl.when(pl.program_id(2) == 0)
    def _():
        st_sc[...] = sin_ref[...]

    u = jnp.concatenate([u_ref[pl.ds(t, rows, stride=S5_TOK), :] for t in range(S5_TOK)], axis=1)
    ub = u.astype(BF16)
    z = _dot(ub, q_ref[...])
    y_in = _dot(ub, m_ref[...]) + d_ref[...] * u

    row = lax.broadcasted_iota(jnp.int32, (rows, half), 0)
    sub = row % 8
    xr = z[:, :half]
    xi = z[:, half:]
    s = 1
    while s < min(8, rows):
        ar = lr_ref[s - 1:s, :]
        ai = li_ref[s - 1:s, :]
        pr = jnp.where(sub >= s, pltpu.roll(xr, s, 0), 0.0)
        pi = jnp.where(sub >= s, pltpu.roll(xi, s, 0), 0.0)
        xr, xi = xr + (ar * pr - ai * pi), xi + (ar * pi + ai * pr)
        s *= 2
    tr = lr_ref[...]
    ti = li_ref[...]
    cr = st_sc[0:1, :]
    ci = st_sc[1:2, :]
    out_r, out_i = [], []
    for blk in range(rows // 8):
        br = xr[blk * 8:(blk + 1) * 8] + (tr * cr - ti * ci)
        bi = xi[blk * 8:(blk + 1) * 8] + (tr * ci + ti * cr)
        out_r.append(br)
        out_i.append(bi)
        cr = br[7:8]
        ci = bi[7:8]
    xr = jnp.concatenate(out_r, axis=0)
    xi = jnp.concatenate(out_i, axis=0)
    sr = st_sc[0:1, :]
    si = st_sc[1:2, :]
    st_sc[0:1, :] = cr
    st_sc[1:2, :] = ci
    sout_ref[...] = st_sc[...]
    x0 = jnp.concatenate([jnp.where(row == 0, sr, pltpu.roll(xr, 1, 0)),
                          jnp.where(row == 0, si, pltpu.roll(xi, 1, 0))], axis=1)
    ya = _gelu_tanh(y_in + _dot(_bf(x0), p_ref[...]))
    for t in range(S5_TOK):
        y_ref[pl.ds(t, rows, stride=S5_TOK), :] = ya[:, t * LANES:(t + 1) * LANES]


def _s5(u, state_in, tab, rows):
    n_blk, bsz, t_len, _ = u.shape
    tm = rows * S5_TOK
    blk = lambda a: pl.BlockSpec((None,) + a.shape[1:], lambda b, k, i: (k,) + (0,) * (a.ndim - 1))
    tokens = pl.BlockSpec((None, None, tm, LANES), lambda b, k, i: (k, b, i, 0))
    weights = [tab["m"], tab["q"], tab["p"], tab["lr"], tab["li"], tab["d"]]
    return pl.pallas_call(
        functools.partial(_s5_kernel, rows=rows),
        grid=(bsz, n_blk, t_len // tm),
        in_specs=[tokens, blk(state_in)] + [blk(a) for a in weights],
        out_specs=[tokens, pl.BlockSpec((None, None, 2, S5_BLOCK_STATE), lambda b, k, i: (b, k, 0, 0))],
        out_shape=[jax.ShapeDtypeStruct(u.shape, F32),
                   jax.ShapeDtypeStruct((bsz, n_blk, 2, S5_BLOCK_STATE), F32)],
        scratch_shapes=[pltpu.VMEM((2, S5_BLOCK_STATE), F32)],
        compiler_params=_params("parallel", "arbitrary", "arbitrary"),
        name="s5",
    )(u, state_in, *weights)


def _s5_tables(lam_re, lam_im, log_dt, b_re, b_im, c_re, c_im, d):
    f = lambda a: a.astype(F32)
    re = jnp.minimum(f(lam_re), -1e-4)
    im = f(lam_im)
    dt = jnp.exp(f(log_dt))[:, None]

    def power(m):
        mag = jnp.exp(m * (re * dt))
        ang = m * (im * dt)
        return mag * jnp.cos(ang), mag * jnp.sin(ang)

    l1r, l1i = power(1.0)
    den = re * re + im * im
    cr = ((l1r - 1.0) * re + l1i * im) / den
    ci = (l1i * re - (l1r - 1.0) * im) / den
    bbr = cr[..., None] * f(b_re) - ci[..., None] * f(b_im)
    bbi = cr[..., None] * f(b_im) + ci[..., None] * f(b_re)
    pw = [power(float(m)) for m in range(S5_TOK + 1)]
    pwr = jnp.stack([x[0] for x in pw])
    pwi = jnp.stack([x[1] for x in pw])
    ccr, cci = f(c_re), f(c_im)
    n_blk = S5_WIDTH // LANES
    gpb = LANES // S5_GROUP
    big = S5_TOK * LANES

    def widen(src, row_grp, col_grp, sub):
        ci = lax.broadcasted_iota(jnp.int32, (LANES, big), 0)
        cj = lax.broadcasted_iota(jnp.int32, (LANES, big), 1)
        spread = ((ci // sub == cj // (sub * gpb)) & (ci % sub == cj % sub)).astype(BF16)
        ri = lax.broadcasted_iota(jnp.int32, (big, big), 0)
        rj = lax.broadcasted_iota(jnp.int32, (big, big), 1)
        keep = ((ri // row_grp) % gpb == (rj // col_grp) % gpb).astype(F32)
        wide = jnp.einsum("brk,kc->brc", src.astype(BF16), spread, preferred_element_type=F32)
        return (wide * keep[None]).astype(BF16)

    tok = S5_TOK
    bbr_t, bbi_t = bbr.transpose(0, 2, 1), bbi.transpose(0, 2, 1)
    ccr_t, cci_t = ccr.transpose(0, 2, 1), cci.transpose(0, 2, 1)
    pwr_g, pwi_g = pwr.transpose(1, 0, 2), pwi.transpose(1, 0, 2)

    pbr = pwr_g[:, :tok, None, :] * bbr_t[:, None] - pwi_g[:, :tok, None, :] * bbi_t[:, None]
    pbi = pwr_g[:, :tok, None, :] * bbi_t[:, None] + pwi_g[:, :tok, None, :] * bbr_t[:, None]
    kern = (jnp.einsum("gon,gmhn->ghmo", ccr, pbr, precision=HI)
            - jnp.einsum("gon,gmhn->ghmo", cci, pbi, precision=HI))
    xi = lax.broadcasted_iota(jnp.int32, (tok, LANES, LANES), 1)
    xc = lax.broadcasted_iota(jnp.int32, (tok, LANES, LANES), 2)
    xt = lax.broadcasted_iota(jnp.int32, (tok, LANES, LANES), 0)
    shift = ((xi % S5_GROUP == xc % S5_GROUP) & (xi // S5_GROUP == xc // S5_GROUP - xt)).astype(BF16)
    toep = jnp.einsum("ghx,ixc->ghic", kern.reshape(kern.shape[0], S5_GROUP, LANES).astype(BF16), shift,
                      preferred_element_type=F32)
    m5 = toep.reshape(n_blk, gpb * S5_GROUP, tok, LANES).transpose(0, 2, 1, 3)
    m_big = widen(m5.reshape(n_blk, big, LANES), S5_GROUP, S5_GROUP, S5_GROUP)

    q5 = jnp.concatenate([pbr[:, ::-1], pbi[:, ::-1]], axis=-1)
    q5 = q5.reshape(n_blk, gpb, tok, S5_GROUP, LANES).transpose(0, 2, 1, 3, 4)
    q_big = widen(q5.reshape(n_blk, big, LANES), S5_GROUP, S5_STATE, S5_STATE)
    ei = lax.broadcasted_iota(jnp.int32, (S5_GROUP, LANES), 0)
    ec = lax.broadcasted_iota(jnp.int32, (S5_GROUP, LANES), 1)
    tile_h = (ei == ec % S5_GROUP).astype(F32)
    rep_t = (ei[:tok] == ec[:tok] // S5_GROUP).astype(F32)
    c_w = lambda a: jnp.einsum("gnh,hc->gnc", a, tile_h, precision=HI)
    p_w = lambda a: jnp.einsum("gtn,tc->gnc", a[:, 1:], rep_t, precision=HI)
    cpr = c_w(ccr_t) * p_w(pwr_g) - c_w(cci_t) * p_w(pwi_g)
    cpi = c_w(ccr_t) * p_w(pwi_g) + c_w(cci_t) * p_w(pwr_g)
    p5 = jnp.stack([cpr, -cpi]).reshape(2, n_blk, gpb * S5_STATE, LANES).transpose(1, 0, 2, 3)
    p_big = widen(p5.reshape(n_blk, big, LANES), S5_STATE, S5_GROUP, S5_GROUP)

    lv = [power(float(S5_TOK * (rr + 1))) for rr in range(8)]
    lr = jnp.stack([x[0].reshape(n_blk, -1) for x in lv], axis=1)
    li = jnp.stack([x[1].reshape(n_blk, -1) for x in lv], axis=1)
    d_cols = jnp.broadcast_to(f(d).reshape(n_blk, 1, 1, LANES), (n_blk, 1, S5_TOK, LANES)).reshape(n_blk, 1, -1)
    return {"m": m_big, "q": q_big, "p": p_big, "lr": lr, "li": li, "d": d_cols}


def _mix_kernel(h_ref, gate_ref, ya_ref, yb_ref, wbr_ref, wglu_ref, wout_ref, gpost_ref, gpre_ref,
                h1_ref, hn_ref, *, tm):
    n_sub = 4 if tm % 1024 == 0 else 1
    sub = tm // n_sub
    rws = [slice(q * sub, (q + 1) * sub) for q in range(n_sub)]
    y_a = [_dot(ya_ref[r, :], wbr_ref[...]) for r in rws]
    yb = [jnp.concatenate([yb_ref[blk, r, :] for blk in range(S5_WIDTH // LANES)], axis=1).astype(BF16)
          for r in rws]
    z = [_dot(x, wglu_ref[...]) for x in yb]
    mix = []
    for q, r in enumerate(rws):
        y_b = z[q][:, :D_MODEL] * jax.nn.sigmoid(z[q][:, D_MODEL:])
        gates = gate_ref[r, :].astype(F32)
        mix.append((gates[:, :D_MODEL] * y_a[q] + gates[:, D_MODEL:] * y_b).astype(BF16))
    o = [_dot(m, wout_ref[...]) for m in mix]
    for q, r in enumerate(rws):
        h1 = h_ref[r, :] + _rms(o[q], gpost_ref[...])
        h1_ref[r, :] = h1
        hn_ref[r, :] = _rms(h1, gpre_ref[...]).astype(BF16)


def _mix(h, gates, ya, yb, prm, tm):
    bsz, t_len, _ = h.shape
    row = lambda c: pl.BlockSpec((None, tm, c), lambda b, i: (b, i, 0))
    full = lambda a: pl.BlockSpec(a.shape, lambda b, i: (0,) * a.ndim)
    weights = [prm["w_branch"], prm["w_glu"], prm["w_out"], prm["g_post"], prm["g_ffn_pre"]]
    return pl.pallas_call(
        functools.partial(_mix_kernel, tm=tm),
        grid=(bsz, t_len // tm),
        in_specs=[row(D_MODEL), row(2 * D_MODEL), row(RWKV_WIDTH), _lane_blocks(tm)] + [_resident(a) for a in weights],
        out_specs=[row(D_MODEL), row(D_MODEL)],
        out_shape=[jax.ShapeDtypeStruct((bsz, t_len, D_MODEL), F32),
                   jax.ShapeDtypeStruct((bsz, t_len, D_MODEL), BF16)],
        compiler_params=_params("parallel", "parallel"),
        name="mix",
    )(h, gates, ya, yb, *weights)


def _ffn_kernel(hn_ref, h1_ref, halo_ref, wup_ref, cw_ref, cb_ref, wd_ref, gpost_ref,
                out_ref, halo_out_ref, halo_sc, *, tm, tf):
    @pl.when(pl.program_id(1) == 0)
    def _():
        halo_sc[...] = halo_ref[...]

    n_sub = 2 if tm % 512 == 0 else 1
    sub = tm // n_sub
    rws = [slice(q * sub, (q + 1) * sub) for q in range(n_sub)]
    row = lax.broadcasted_iota(jnp.int32, (sub, tf), 0)
    acc = [None] * n_sub
    for f in range(D_FF // tf):
        cols = slice(f * tf, (f + 1) * tf)
        hn = [hn_ref[r, :] for r in rws]
        ua = [_dot(x, wup_ref[:, cols]) for x in hn]
        ub = [_dot(x, wup_ref[:, D_FF + f * tf:D_FF + (f + 1) * tf]) for x in hn]
        tails = [halo_sc[:, cols]] + [x[sub - 8:, :] for x in ua]
        act = []
        for q in range(n_sub):
            prev1 = tails[q][7:8, :]
            prev2 = tails[q][6:7, :]
            u1 = jnp.where(row == 0, prev1, pltpu.roll(ua[q], 1, 0))
            u2 = jnp.where(row == 0, prev2, jnp.where(row == 1, prev1, pltpu.roll(ua[q], 2, 0)))
            conv = cw_ref[0:1, cols] * u2 + cw_ref[1:2, cols] * u1 + cw_ref[2:3, cols] * ua[q] + cb_ref[:, cols]
            act.append((_gelu_tanh(conv) * ub[q]).astype(BF16))
        halo_sc[:, cols] = tails[n_sub]
        part = [_dot(x, wd_ref[cols, :]) for x in act]
        acc = [part[q] if acc[q] is None else acc[q] + part[q] for q in range(n_sub)]
    halo_out_ref[...] = halo_sc[...]
    for q, r in enumerate(rws):
        out_ref[r, :] = h1_ref[r, :] + _rms(acc[q], gpost_ref[...])


def _ffn(hn, h1, halo_in, prm, tm, tf):
    bsz, t_len, _ = hn.shape
    n_t = t_len // tm
    row = lambda c: pl.BlockSpec((None, tm, c), lambda b, i: (b, i, 0))
    weights = [prm["w_up"], prm["conv_w"], prm["conv_b"], prm["w_down"], prm["g_ffn_post"]]
    return pl.pallas_call(
        functools.partial(_ffn_kernel, tm=tm, tf=tf),
        grid=(bsz, n_t),
        in_specs=[row(D_MODEL), row(D_MODEL), _resident(halo_in)] + [_resident(a) for a in weights],
        out_specs=[row(D_MODEL), pl.BlockSpec((None, None, 8, D_FF), lambda b, i: (b, i, 0, 0))],
        out_shape=[jax.ShapeDtypeStruct((bsz, t_len, D_MODEL), F32),
                   jax.ShapeDtypeStruct((bsz, n_t, 8, D_FF), F32)],
        scratch_shapes=[pltpu.VMEM((8, D_FF), F32)],
        compiler_params=_params("parallel", "arbitrary"),
        name="ffn",
    )(hn, h1, halo_in, *weights)


def _pick(t_len, options):
    for o in options:
        if t_len % o == 0:
            return o
    raise ValueError(f"no tile for sequence length {t_len}")


def _block(h, carry, prm, tab, n_pad=0):
    t_len = h.shape[1]
    keep = (jnp.arange(t_len) >= n_pad)[None, :, None]
    tm_proj = _pick(t_len, (512, 64))
    tm_rwkv = _pick(t_len, (256, 64))
    rows_s5 = _pick(t_len // S5_TOK, (512, 8))
    tm_mix = _pick(t_len, (1024, 64))
    tm_ffn = _pick(t_len, (512, 64))
    *rwkv_ops, u, gates, last_rows = _inproj(h, carry["prev_row"], prm, tm_proj)
    ya, rwkv_state = _rwkv(rwkv_ops, carry["rwkv_state"], prm, tm_rwkv)
    yb, s5_state = _s5(u, carry["s5_state"], tab, rows_s5)
    h1, hn2 = _mix(h, gates, ya, yb, prm, tm_mix)
    if n_pad:
        hn2 = jnp.where(keep, hn2, jnp.zeros_like(hn2))
    out, halo = _ffn(hn2, h1, carry["halo"], prm, tm_ffn, D_FF)
    if n_pad:
        out = jnp.where(keep, out, jnp.zeros_like(out))
    new_carry = {"prev_row": last_rows[0, -1], "rwkv_state": rwkv_state[0], "s5_state": s5_state[0],
                 "halo": halo[0, -1]}
    return out, new_carry


def kernel(x, meta_tokens, norm_mix_pre, norm_mix_post, w_in, rwkv_shift_mu, rwkv_w0, rwkv_w_up, rwkv_a0, rwkv_a_up, rwkv_g_up, rwkv_k_k, rwkv_k_a, rwkv_r_k, rwkv_ln_gain, rwkv_ln_bias, rwkv_w_branch, s5_lambda_re, s5_lambda_im, s5_log_dt, s5_b_re, s5_b_im, s5_c_re, s5_c_im, s5_d, s5_w_glu, w_out, norm_ffn_pre, norm_ffn_post, ffn_w_up, ffn_conv_w, ffn_conv_b, ffn_w_down):
    depth = w_in.shape[0]
    n_pairs = RWKV_WIDTH // LANES
    zeros_lora = jnp.zeros((W_LORA, RWKV_WIDTH), F32)

    h_meta = jnp.concatenate([jnp.zeros((META_PAD - N_META, D_MODEL), x.dtype), meta_tokens.astype(x.dtype)])[None]
    h_main = x
    for l in range(depth):
        vec = lambda a: a[l].reshape(1, -1).astype(F32)
        prm = {
            "g_pre": vec(norm_mix_pre), "g_post": vec(norm_mix_post),
            "g_ffn_pre": vec(norm_ffn_pre), "g_ffn_post": vec(norm_ffn_post),
            "w_in": w_in[l].astype(BF16),
            "mu": vec(rwkv_shift_mu), "w0": vec(rwkv_w0), "a0": vec(rwkv_a0),
            "wup": jnp.concatenate([rwkv_w_up[l].astype(F32), zeros_lora]),
            "aup": jnp.concatenate([zeros_lora, rwkv_a_up[l].astype(F32)]),
            "gup": rwkv_g_up[l].astype(F32),
            "k_k": vec(rwkv_k_k), "k_a": vec(rwkv_k_a), "r_k": vec(rwkv_r_k),
            "ln_g": vec(rwkv_ln_gain), "ln_b": vec(rwkv_ln_bias),
            "w_branch": rwkv_w_branch[l].astype(BF16), "w_glu": s5_w_glu[l].astype(BF16),
            "w_out": w_out[l].astype(BF16), "w_up": ffn_w_up[l].astype(BF16),
            "conv_w": ffn_conv_w[l].astype(F32), "conv_b": vec(ffn_conv_b),
            "w_down": ffn_w_down[l].astype(BF16),
        }
        tab = _s5_tables(s5_lambda_re[l], s5_lambda_im[l], s5_log_dt[l], s5_b_re[l],
                         s5_b_im[l], s5_c_re[l], s5_c_im[l], s5_d[l])
        zero_carry = {"prev_row": jnp.zeros((1, RWKV_PROJ), F32),
                      "rwkv_state": jnp.zeros((n_pairs, LANES, LANES), F32),
                      "s5_state": jnp.zeros((S5_WIDTH // LANES, 2, S5_BLOCK_STATE), F32),
                      "halo": jnp.zeros((8, D_FF), F32)}
        h_meta, carry = _block(h_meta, zero_carry, prm, tab, n_pad=META_PAD - N_META)
        h_main, _ = _block(h_main, carry, prm, tab)
    return h_main
```

```python
import functools
import math

import jax
import jax.numpy as jnp
from jax import lax
from jax.experimental import pallas as pl
from jax.experimental.pallas import tpu as pltpu

F32 = jnp.float32
BF16 = jnp.bfloat16
HI = lax.Precision.HIGHEST

D_MODEL = 1024
N_META = 16
RWKV_WIDTH = 512
HEAD_DIM = 64
W_LORA = 64
A_LORA = 64
G_LORA = 128
RWKV_PROJ = 3 * RWKV_WIDTH + W_LORA + A_LORA + G_LORA
GN_EPS = 64e-5
S5_WIDTH = 512
S5_GROUP = 16
S5_STATE = 64
D_FF = 2816
RMS_EPS = 1e-6

LANES = 128
CHUNK = 64
CHUNK_GROUP = 4
S5_TOK = 8
S5_BLOCK_STATE = (LANES // S5_GROUP) * S5_STATE
META_PAD = 64
VMEM_LIMIT = 56 * 1024 * 1024


def _dot(a, b, precision=None):
    return jnp.dot(a, b, preferred_element_type=F32, precision=precision)


def _dot_nt(a, b):
    return lax.dot_general(a, b, (((1,), (1,)), ((), ())), preferred_element_type=F32)


def _dot_tn(a, b):
    return lax.dot_general(a, b, (((0,), (0,)), ((), ())), preferred_element_type=F32)


def _bf(x):
    return x.astype(BF16)


def _split(x):
    hi = x.astype(BF16)
    return hi, (x - hi.astype(F32)).astype(BF16)


def _dot_x3(a, b):
    ah, al = _split(a)
    bh, bl = _split(b)
    return _dot(ah, bh) + _dot(al, bh) + _dot(ah, bl)


def _gelu_tanh(x):
    c = math.sqrt(2.0 / math.pi)
    return 0.5 * x * (1.0 + jnp.tanh(c * (x + 0.044715 * (x * x * x))))


def _softplus(x):
    return jnp.maximum(x, 0.0) + jnp.log(1.0 + jnp.exp(-jnp.abs(x)))


def _rms(x, g):
    ms = jnp.mean(x * x, axis=-1, keepdims=True)
    return x * lax.rsqrt(ms + RMS_EPS) * g


def _params(*sem):
    return pltpu.CompilerParams(dimension_semantics=sem, vmem_limit_bytes=VMEM_LIMIT)


def _resident(a):
    return pl.BlockSpec(a.shape, lambda b, i: (0,) * a.ndim, pipeline_mode=pl.Buffered(1))


def _lane_blocks(tm):
    return pl.BlockSpec((S5_WIDTH // LANES, None, tm, LANES), lambda b, i: (0, b, i, 0))


def _inproj_kernel(h_ref, prev_ref, g_ref, w_ref, mu_ref, w0_ref, wup_ref, a0_ref, aup_ref, gup_ref,
                   logw_ref, r_ref, k_ref, v_ref, a_ref, og_ref, u_ref, gate_ref, last_ref,
                   prev_sc, *, tm):
    @pl.when(pl.program_id(1) == 0)
    def _():
        prev_sc[...] = prev_ref[...]

    n_sub = 2 if tm % 512 == 0 else 1
    sub = tm // n_sub
    rws = [slice(q * sub, (q + 1) * sub) for q in range(n_sub)]
    subs = range(n_sub)
    hn = [_rms(h_ref[r, :], g_ref[...]).astype(BF16) for r in rws]
    n_gate = gate_ref.shape[-1]
    gate_cols = n_gate // 4

    def gate_part(q):
        lo = RWKV_PROJ + S5_WIDTH + q * gate_cols
        for i in subs:
            gate_ref[rws[i], q * gate_cols:(q + 1) * gate_cols] = jax.nn.sigmoid(
                _dot(hn[i], w_ref[:, lo:lo + gate_cols])).astype(BF16)

    p = [_dot(x, w_ref[:, 0:RWKV_PROJ]) for x in hn]
    row = lax.broadcasted_iota(jnp.int32, p[0].shape, 0)
    lasts = [prev_sc[...]] + [x[sub - 1:, :] for x in p]
    prev_sc[...] = lasts[n_sub]
    last_ref[...] = lasts[n_sub]
    pm = []
    for i in subs:
        ps = jnp.where(row == 0, lasts[i], pltpu.roll(p[i], 1, 0))
        pm.append(p[i] + (ps - p[i]) * mu_ref[...])
    for i in subs:
        r_ref[rws[i], :] = pm[i][:, 0:RWKV_WIDTH].astype(BF16)
        k_ref[rws[i], :] = pm[i][:, RWKV_WIDTH:2 * RWKV_WIDTH].astype(BF16)
        v_ref[rws[i], :] = pm[i][:, 2 * RWKV_WIDTH:3 * RWKV_WIDTH].astype(BF16)
    gate_part(0)
    wa = [x[:, 3 * RWKV_WIDTH:3 * RWKV_WIDTH + W_LORA + A_LORA] for x in pm]
    gd = [x[:, 3 * RWKV_WIDTH + W_LORA + A_LORA:] for x in pm]
    for i in subs:
        w_log = -_softplus(-(w0_ref[...] + _dot_x3(jnp.tanh(wa[i]), wup_ref[...]))) - 0.5
        logw_ref[rws[i], :] = -jnp.exp(w_log)
    gate_part(1)
    for i in subs:
        a_ref[rws[i], :] = jax.nn.sigmoid(a0_ref[...] + _dot(_bf(wa[i]), _bf(aup_ref[...]))).astype(BF16)
        og_ref[rws[i], :] = _dot(_bf(jax.nn.sigmoid(gd[i])), _bf(gup_ref[...])).astype(BF16)
    gate_part(2)
    gate_part(3)
    for i in subs:
        u = _dot(hn[i], w_ref[:, RWKV_PROJ:RWKV_PROJ + S5_WIDTH])
        for blk in range(S5_WIDTH // LANES):
            u_ref[blk, rws[i], :] = u[:, blk * LANES:(blk + 1) * LANES]


def _inproj(h, prev_row, prm, tm):
    bsz, t_len, _ = h.shape
    n_t = t_len // tm
    w = prm["w_in"]
    n_gate = w.shape[1] - RWKV_PROJ - S5_WIDTH
    row = lambda c: pl.BlockSpec((None, tm, c), lambda b, i: (b, i, 0))
    weights = [prm["g_pre"], w, prm["mu"], prm["w0"], prm["wup"], prm["a0"], prm["aup"], prm["gup"]]
    act = lambda dt: jax.ShapeDtypeStruct((bsz, t_len, RWKV_WIDTH), dt)
    return pl.pallas_call(
        functools.partial(_inproj_kernel, tm=tm),
        grid=(bsz, n_t),
        in_specs=[row(D_MODEL), _resident(prev_row)] + [_resident(a) for a in weights],
        out_specs=[row(RWKV_WIDTH)] * 6 + [_lane_blocks(tm), row(n_gate),
                                           pl.BlockSpec((None, None, 1, RWKV_PROJ), lambda b, i: (b, i, 0, 0))],
        out_shape=[act(F32)] + [act(BF16)] * 5
                  + [jax.ShapeDtypeStruct((S5_WIDTH // LANES, bsz, t_len, LANES), F32),
                     jax.ShapeDtypeStruct((bsz, t_len, n_gate), BF16),
                     jax.ShapeDtypeStruct((bsz, n_t, 1, RWKV_PROJ), F32)],
        scratch_shapes=[pltpu.VMEM((1, RWKV_PROJ), F32)],
        compiler_params=_params("parallel", "arbitrary"),
        name="inproj",
    )(h, prev_row, *weights)


def _rwkv_kernel(logw_ref, r_ref, k_ref, v_ref, a_ref, og_ref, sin_ref, kk_ref, ka_ref, rk_ref, lng_ref, lnb_ref,
                 y_ref, sout_ref,
                 state_sc, at_sc, rt_sc, bh_sc, kh_sc, bl_sc, kl_sc, gl_sc, bon_sc, rp_sc, yl_sc, phi_sc, psi_sc, *, tm):
    n_b = y_ref.shape[0]
    L = CHUNK
    n_chunks = tm // L
    n_pairs = RWKV_WIDTH // LANES
    n2 = 2 * L

    @pl.when(pl.program_id(1) == 0)
    def _():
        for bi in range(n_b):
            state_sc[bi] = sin_ref[...]

    si = lax.broadcasted_iota(jnp.int32, (n2, n2), 0)
    sj = lax.broadcasted_iota(jnp.int32, (n2, n2), 1)
    same_head = (si // L) == (sj // L)
    strict = same_head & ((si % L) > (sj % L))
    incl = same_head & ((si % L) >= (sj % L))
    eye_mask = si == sj
    eye = eye_mask.astype(F32)
    head_sum = same_head.astype(BF16)
    head0 = lax.broadcasted_iota(jnp.int32, (L, LANES), 1) < HEAD_DIM

    li = lax.broadcasted_iota(jnp.int32, (L, L), 0)
    lj = lax.broadcasted_iota(jnp.int32, (L, L), 1)
    ltri = (li >= lj).astype(BF16)
    for bi, j in [(bi, j) for bi in range(n_b) for j in range(n_pairs)]:
        cols = slice(j * LANES, (j + 1) * LANES)
        k_j = k_ref[bi, :, cols].astype(F32)
        a_j = a_ref[bi, :, cols].astype(F32)
        r_j = r_ref[bi, :, cols].astype(F32)
        kx = k_j * kk_ref[:, cols]
        kkn_j = kx * lax.rsqrt(jnp.maximum(_dot(_bf(kx * kx), head_sum), 1e-24))
        scale = 1.0 + (a_j - 1.0) * ka_ref[:, cols]
        kp_j = k_j * scale
        b_j = kkn_j * a_j
        bon_sc[bi, :, cols] = (_dot(_bf(r_j * kp_j * rk_ref[:, cols]), head_sum)
                               * v_ref[bi, :, cols].astype(F32))
        for c in range(n_chunks):
            rows = slice(c * L, (c + 1) * L)
            lw = logw_ref[bi, rows, cols]
            lw_hi, lw_lo = _split(lw)
            cum = _dot(ltri, lw_hi) + _dot(ltri, lw_lo)
            cum_l = jnp.sum(lw, axis=0, keepdims=True)
            b = b_j[rows]
            kp = kp_j[rows]
            at_sc[bi, rows, cols] = -kkn_j[rows] * jnp.exp(cum - lw)
            rt_sc[bi, rows, cols] = r_j[rows] * jnp.exp(cum)
            ginv = jnp.exp(-cum)
            bh_sc[bi, rows, cols] = b * ginv
            kh_sc[bi, rows, cols] = kp * ginv
            g_l = jnp.exp(cum_l)
            tail = g_l * ginv
            bl_sc[bi, rows, cols] = b * tail
            kl_sc[bi, rows, cols] = kp * tail
            gl_sc[bi, c * 8:(c + 1) * 8, cols] = jnp.broadcast_to(g_l, (8, LANES))

    def split_heads(x):
        return jnp.concatenate([jnp.where(head0, x, 0.0), jnp.where(head0, 0.0, x)], axis=0)

    def pick_heads(xs):
        return jnp.where(head0, xs[:L], xs[L:])

    def merge_heads(xs):
        return xs[:L] + xs[L:]

    group = min(CHUNK_GROUP, n_chunks)

    def local_body(cg, bi):
        units = [(cg * group + dc, j) for dc in range(group) for j in range(n_pairs)]
        n_u = range(len(units))
        rws = [pl.ds(pl.multiple_of(c * L, L), L) for c, _ in units]
        cl = [slice(j * LANES, (j + 1) * LANES) for _, j in units]
        at_s = [split_heads(at_sc[bi, rws[i], cl[i]]) for i in n_u]
        rt_s = [split_heads(rt_sc[bi, rws[i], cl[i]]) for i in n_u]
        bl = [_bf(bl_sc[bi, rws[i], cl[i]]) for i in n_u]
        kl = [_bf(kl_sc[bi, rws[i], cl[i]]) for i in n_u]
        vv = [v_ref[bi, rws[i], cl[i]] for i in n_u]
        gl = [gl_sc[bi, pl.ds(pl.multiple_of(units[i][0] * 8, 8), 1), cl[i]] for i in n_u]
        bk_s = [_bf(jnp.concatenate([split_heads(bh_sc[bi, rws[i], cl[i]]), split_heads(kh_sc[bi, rws[i], cl[i]])],
                                    axis=0)) for i in n_u]
        at_b = [_bf(x) for x in at_s]
        lhs = [jnp.concatenate([at_b[i], _bf(rt_s[i])], axis=0) for i in n_u]
        abk = [_dot_nt(lhs[i], bk_s[i]) for i in n_u]
        a_ab = [jnp.where(strict, abk[i][:n2, :n2], 0.0) for i in n_u]
        a_ak = [_bf(jnp.where(strict, abk[i][:n2, n2:], 0.0)) for i in n_u]
        a_rb = [_bf(jnp.where(incl, abk[i][n2:, :n2], 0.0)) for i in n_u]
        a_rk = [_bf(jnp.where(incl, abk[i][n2:, n2:], 0.0)) for i in n_u]
        first = ((si // 2) == (sj // 2)) & ((si % 2) == 1) & ((sj % 2) == 0)
        tinv = [eye + jnp.where(first, a_ab[i], 0.0) for i in n_u]
        s = 2
        while s < L:
            off = ((si // (2 * s)) == (sj // (2 * s))) & ((si % (2 * s)) >= s) & ((sj % (2 * s)) < s)
            t_b = [_bf(t) for t in tinv]
            x1 = [_dot(_bf(jnp.where(off, a_ab[i], 0.0)), t_b[i]) for i in n_u]
            x2 = [_dot(t_b[i], _bf(x1[i])) for i in n_u]
            tinv = [tinv[i] + x2[i] for i in n_u]
            s *= 2
        v2 = [jnp.concatenate([vv[i], vv[i]], axis=0) for i in n_u]
        av = [_dot(a_ak[i], v2[i]) for i in n_u]
        tx = [_dot(_bf(tinv[i]), jnp.concatenate([at_b[i], _bf(av[i])], axis=1)) for i in n_u]
        g1 = [_dot(a_rb[i], _bf(tx[i])) for i in n_u]
        g2 = [_dot(a_rk[i], v2[i]) for i in n_u]
        w = [_bf(merge_heads(tx[i][:, :LANES])) for i in n_u]
        uv = [jnp.concatenate([_bf(pick_heads(tx[i][:, LANES:])), vv[i]], axis=0) for i in n_u]
        phi = [_dot_tn(w[i], bl[i]) for i in n_u]
        psi = [_dot_tn(uv[i], jnp.concatenate([bl[i], kl[i]], axis=0)) for i in n_u]
        for i in n_u:
            c, j = units[i]
            rp_sc[bi, rws[i], cl[i]] = merge_heads(rt_s[i] + g1[i][:, :LANES])
            yl_sc[bi, rws[i], cl[i]] = pick_heads(g1[i][:, LANES:] + g2[i])
            phi_sc[bi, c, j] = jnp.where(same_head, phi[i], 0.0) + jnp.where(eye_mask, gl[i], 0.0)
            psi_sc[bi, c, j] = jnp.where(same_head, psi[i], 0.0)
        return bi

    for bi in range(n_b):
        lax.fori_loop(0, n_chunks // group, local_body, bi)

    def state_body(c, carry):
        rws = pl.ds(pl.multiple_of(c * L, L), L)
        chains = [(bi, j) for bi in range(n_b) for j in range(n_pairs)]
        n_c = range(len(chains))
        cl = [slice(j * LANES, (j + 1) * LANES) for _, j in chains]
        st = [_split(state_sc[bi, j]) for bi, j in chains]
        ph = [_split(phi_sc[bi, c, j]) for bi, j in chains]
        rp = [_bf(rp_sc[chains[i][0], rws, cl[i]]) for i in n_c]
        y0 = [_dot_nt(rp[i], st[i][0]) for i in n_c]
        n0 = [_dot(st[i][0], ph[i][0]) for i in n_c]
        n1 = [_dot(st[i][1], ph[i][0]) for i in n_c]
        n2_ = [_dot(st[i][0], ph[i][1]) for i in n_c]
        for i in n_c:
            bi, j = chains[i]
            rp_sc[bi, rws, cl[i]] = y0[i] + yl_sc[bi, rws, cl[i]]
            state_sc[bi, j] = n0[i] + n1[i] + n2_[i] + psi_sc[bi, c, j]
        return carry

    lax.fori_loop(0, n_chunks, state_body, 0)

    inv_n = 1.0 / HEAD_DIM
    for bi, j in [(bi, j) for bi in range(n_b) for j in range(n_pairs)]:
        cols = slice(j * LANES, (j + 1) * LANES)
        y = rp_sc[bi, :, cols]
        yc = y - _dot(_bf(y), head_sum) * inv_n
        var = _dot(_bf(yc * yc), head_sum) * inv_n
        yn = yc * lax.rsqrt(var + GN_EPS) * lng_ref[:, cols] + lnb_ref[:, cols]
        y_ref[bi, :, cols] = ((yn + bon_sc[bi, :, cols]) * og_ref[bi, :, cols].astype(F32)).astype(BF16)
    sout_ref[...] = state_sc[...]


def _rwkv(ops, state_in, prm, tm):
    bsz, t_len, _ = ops[0].shape
    n_pairs = RWKV_WIDTH // LANES
    n_chunks = tm // CHUNK
    full = lambda a: pl.BlockSpec(a.shape, lambda b, i: (0,) * a.ndim)
    n_b = 2 if bsz % 2 == 0 else 1
    row = pl.BlockSpec((n_b, tm, RWKV_WIDTH), lambda b, i: (b, i, 0))
    weights = [prm["k_k"], prm["k_a"], prm["r_k"], prm["ln_g"], prm["ln_b"]]
    tile = pltpu.VMEM((n_b, tm, RWKV_WIDTH), F32)
    mats = pltpu.VMEM((n_b, n_chunks, n_pairs, LANES, LANES), F32)
    return pl.pallas_call(
        functools.partial(_rwkv_kernel, tm=tm),
        grid=(bsz // n_b, t_len // tm),
        in_specs=[row] * len(ops) + [full(state_in)] + [full(a) for a in weights],
        out_specs=[row, pl.BlockSpec((n_b, n_pairs, LANES, LANES), lambda b, i: (b, 0, 0, 0))],
        out_shape=[jax.ShapeDtypeStruct((bsz, t_len, RWKV_WIDTH), BF16),
                   jax.ShapeDtypeStruct((bsz, n_pairs, LANES, LANES), F32)],
        scratch_shapes=[pltpu.VMEM((n_b, n_pairs, LANES, LANES), F32)] + [tile] * 6
                       + [pltpu.VMEM((n_b, n_chunks * 8, RWKV_WIDTH), F32)] + [tile] * 3 + [mats] * 2,
        compiler_params=_params("parallel", "arbitrary"),
        name="rwkv",
    )(*ops, state_in, *weights)


def _s5_kernel(u_ref, sin_ref, m_ref, q_ref, p_ref, lr_ref, li_ref, d_ref,
               y_ref, sout_ref, st_sc, *, rows):
    half = S5_BLOCK_STATE

    @pl.when(pl.program_id(2) == 0)
    def _():
        st_sc[...] = sin_ref[...]

    u = jnp.concatenate([u_ref[pl.ds(t, rows, stride=S5_TOK), :] for t in range(S5_TOK)], axis=1)
    ub = u.astype(BF16)
    z = _dot(ub, q_ref[...])
    y_in = _dot(ub, m_ref[...]) + d_ref[...] * u

    row = lax.broadcasted_iota(jnp.int32, (rows, half), 0)
    sub = row % 8
    xr = z[:, :half]
    xi = z[:, half:]
    s = 1
    while s < min(8, rows):
        ar = lr_ref[s - 1:s, :]
        ai = li_ref[s - 1:s, :]
        pr = jnp.where(sub >= s, pltpu.roll(xr, s, 0), 0.0)
        pi = jnp.where(sub >= s, pltpu.roll(xi, s, 0), 0.0)
        xr, xi = xr + (ar * pr - ai * pi), xi + (ar * pi + ai * pr)
        s *= 2
    tr = lr_ref[...]
    ti = li_ref[...]
    cr = st_sc[0:1, :]
    ci = st_sc[1:2, :]
    out_r, out_i = [], []
    for blk in range(rows // 8):
        br = xr[blk * 8:(blk + 1) * 8] + (tr * cr - ti * ci)
        bi = xi[blk * 8:(blk + 1) * 8] + (tr * ci + ti * cr)
        out_r.append(br)
        out_i.append(bi)
        cr = br[7:8]
        ci = bi[7:8]
    xr = jnp.concatenate(out_r, axis=0)
    xi = jnp.concatenate(out_i, axis=0)
    sr = st_sc[0:1, :]
    si = st_sc[1:2, :]
    st_sc[0:1, :] = cr
    st_sc[1:2, :] = ci
    sout_ref[...] = st_sc[...]
    x0 = jnp.concatenate([jnp.where(row == 0, sr, pltpu.roll(xr, 1, 0)),
                          jnp.where(row == 0, si, pltpu.roll(xi, 1, 0))], axis=1)
    ya = _gelu_tanh(y_in + _dot(_bf(x0), p_ref[...]))
    for t in range(S5_TOK):
        y_ref[pl.ds(t, rows, stride=S5_TOK), :] = ya[:, t * LANES:(t + 1) * LANES]


def _s5(u, state_in, tab, rows):
    n_blk, bsz, t_len, _ = u.shape
    tm = rows * S5_TOK
    blk = lambda a: pl.BlockSpec((None,) + a.shape[1:], lambda b, k, i: (k,) + (0,) * (a.ndim - 1))
    tokens = pl.BlockSpec((None, None, tm, LANES), lambda b, k, i: (k, b, i, 0))
    weights = [tab["m"], tab["q"], tab["p"], tab["lr"], tab["li"], tab["d"]]
    return pl.pallas_call(
        functools.partial(_s5_kernel, rows=rows),
        grid=(bsz, n_blk, t_len // tm),
        in_specs=[tokens, blk(state_in)] + [blk(a) for a in weights],
        out_specs=[tokens, pl.BlockSpec((None, None, 2, S5_BLOCK_STATE), lambda b, k, i: (b, k, 0, 0))],
        out_shape=[jax.ShapeDtypeStruct(u.shape, F32),
                   jax.ShapeDtypeStruct((bsz, n_blk, 2, S5_BLOCK_STATE), F32)],
        scratch_shapes=[pltpu.VMEM((2, S5_BLOCK_STATE), F32)],
        compiler_params=_params("parallel", "arbitrary", "arbitrary"),
        name="s5",
    )(u, state_in, *weights)


def _s5_tables(lam_re, lam_im, log_dt, b_re, b_im, c_re, c_im, d):
    f = lambda a: a.astype(F32)
    re = jnp.minimum(f(lam_re), -1e-4)
    im = f(lam_im)
    dt = jnp.exp(f(log_dt))[:, None]

    def power(m):
        mag = jnp.exp(m * (re * dt))
        ang = m * (im * dt)
        return mag * jnp.cos(ang), mag * jnp.sin(ang)

    l1r, l1i = power(1.0)
    den = re * re + im * im
    cr = ((l1r - 1.0) * re + l1i * im) / den
    ci = (l1i * re - (l1r - 1.0) * im) / den
    bbr = cr[..., None] * f(b_re) - ci[..., None] * f(b_im)
    bbi = cr[..., None] * f(b_im) + ci[..., None] * f(b_re)
    pw = [power(float(m)) for m in range(S5_TOK + 1)]
    pwr = jnp.stack([x[0] for x in pw])
    pwi = jnp.stack([x[1] for x in pw])
    ccr, cci = f(c_re), f(c_im)
    n_blk = S5_WIDTH // LANES
    gpb = LANES // S5_GROUP
    big = S5_TOK * LANES

    def widen(src, row_grp, col_grp, sub):
        ci = lax.broadcasted_iota(jnp.int32, (LANES, big), 0)
        cj = lax.broadcasted_iota(jnp.int32, (LANES, big), 1)
        spread = ((ci // sub == cj // (sub * gpb)) & (ci % sub == cj % sub)).astype(BF16)
        ri = lax.broadcasted_iota(jnp.int32, (big, big), 0)
        rj = lax.broadcasted_iota(jnp.int32, (big, big), 1)
        keep = ((ri // row_grp) % gpb == (rj // col_grp) % gpb).astype(F32)
        wide = jnp.einsum("brk,kc->brc", src.astype(BF16), spread, preferred_element_type=F32)
        return (wide * keep[None]).astype(BF16)

    tok = S5_TOK
    bbr_t, bbi_t = bbr.transpose(0, 2, 1), bbi.transpose(0, 2, 1)
    ccr_t, cci_t = ccr.transpose(0, 2, 1), cci.transpose(0, 2, 1)
    pwr_g, pwi_g = pwr.transpose(1, 0, 2), pwi.transpose(1, 0, 2)

    pbr = pwr_g[:, :tok, None, :] * bbr_t[:, None] - pwi_g[:, :tok, None, :] * bbi_t[:, None]
    pbi = pwr_g[:, :tok, None, :] * bbi_t[:, None] + pwi_g[:, :tok, None, :] * bbr_t[:, None]
    kern = (jnp.einsum("gon,gmhn->ghmo", ccr, pbr, precision=HI)
            - jnp.einsum("gon,gmhn->ghmo", cci, pbi, precision=HI))
    xi = lax.broadcasted_iota(jnp.int32, (tok, LANES, LANES), 1)
    xc = lax.broadcasted_iota(jnp.int32, (tok, LANES, LANES), 2)
    xt = lax.broadcasted_iota(jnp.int32, (tok, LANES, LANES), 0)
    shift = ((xi % S5_GROUP == xc % S5_GROUP) & (xi // S5_GROUP == xc // S5_GROUP - xt)).astype(BF16)
    toep = jnp.einsum("ghx,ixc->ghic", kern.reshape(kern.shape[0], S5_GROUP, LANES).astype(BF16), shift,
                      preferred_element_type=F32)
    m5 = toep.reshape(n_blk, gpb * S5_GROUP, tok, LANES).transpose(0, 2, 1, 3)
    m_big = widen(m5.reshape(n_blk, big, LANES), S5_GROUP, S5_GROUP, S5_GROUP)

    q5 = jnp.concatenate([pbr[:, ::-1], pbi[:, ::-1]], axis=-1)
    q5 = q5.reshape(n_blk, gpb, tok, S5_GROUP, LANES).transpose(0, 2, 1, 3, 4)
    q_big = widen(q5.reshape(n_blk, big, LANES), S5_GROUP, S5_STATE, S5_STATE)
    ei = lax.broadcasted_iota(jnp.int32, (S5_GROUP, LANES), 0)
    ec = lax.broadcasted_iota(jnp.int32, (S5_GROUP, LANES), 1)
    tile_h = (ei == ec % S5_GROUP).astype(F32)
    rep_t = (ei[:tok] == ec[:tok] // S5_GROUP).astype(F32)
    c_w = lambda a: jnp.einsum("gnh,hc->gnc", a, tile_h, precision=HI)
    p_w = lambda a: jnp.einsum("gtn,tc->gnc", a[:, 1:], rep_t, precision=HI)
    cpr = c_w(ccr_t) * p_w(pwr_g) - c_w(cci_t) * p_w(pwi_g)
    cpi = c_w(ccr_t) * p_w(pwi_g) + c_w(cci_t) * p_w(pwr_g)
    p5 = jnp.stack([cpr, -cpi]).reshape(2, n_blk, gpb * S5_STATE, LANES).transpose(1, 0, 2, 3)
    p_big = widen(p5.reshape(n_blk, big, LANES), S5_STATE, S5_GROUP, S5_GROUP)

    lv = [power(float(S5_TOK * (rr + 1))) for rr in range(8)]
    lr = jnp.stack([x[0].reshape(n_blk, -1) for x in lv], axis=1)
    li = jnp.stack([x[1].reshape(n_blk, -1) for x in lv], axis=1)
    d_cols = jnp.broadcast_to(f(d).reshape(n_blk, 1, 1, LANES), (n_blk, 1, S5_TOK, LANES)).reshape(n_blk, 1, -1)
    return {"m": m_big, "q": q_big, "p": p_big, "lr": lr, "li": li, "d": d_cols}


def _mix_kernel(h_ref, gate_ref, ya_ref, yb_ref, wbr_ref, wglu_ref, wout_ref, gpost_ref, gpre_ref,
                h1_ref, hn_ref, *, tm):
    n_sub = 4 if tm % 1024 == 0 else 1
    sub = tm // n_sub
    rws = [slice(q * sub, (q + 1) * sub) for q in range(n_sub)]
    y_a = [_dot(ya_ref[r, :], wbr_ref[...]) for r in rws]
    yb = [jnp.concatenate([yb_ref[blk, r, :] for blk in range(S5_WIDTH // LANES)], axis=1).astype(BF16)
          for r in rws]
    z = [_dot(x, wglu_ref[...]) for x in yb]
    mix = []
    for q, r in enumerate(rws):
        y_b = z[q][:, :D_MODEL] * jax.nn.sigmoid(z[q][:, D_MODEL:])
        gates = gate_ref[r, :].astype(F32)
        mix.append((gates[:, :D_MODEL] * y_a[q] + gates[:, D_MODEL:] * y_b).astype(BF16))
    o = [_dot(m, wout_ref[...]) for m in mix]
    for q, r in enumerate(rws):
        h1 = h_ref[r, :] + _rms(o[q], gpost_ref[...])
        h1_ref[r, :] = h1
        hn_ref[r, :] = _rms(h1, gpre_ref[...]).astype(BF16)


def _mix(h, gates, ya, yb, prm, tm):
    bsz, t_len, _ = h.shape
    row = lambda c: pl.BlockSpec((None, tm, c), lambda b, i: (b, i, 0))
    full = lambda a: pl.BlockSpec(a.shape, lambda b, i: (0,) * a.ndim)
    weights = [prm["w_branch"], prm["w_glu"], prm["w_out"], prm["g_post"], prm["g_ffn_pre"]]
    return pl.pallas_call(
        functools.partial(_mix_kernel, tm=tm),
        grid=(bsz, t_len // tm),
        in_specs=[row(D_MODEL), row(2 * D_MODEL), row(RWKV_WIDTH), _lane_blocks(tm)] + [_resident(a) for a in weights],
        out_specs=[row(D_MODEL), row(D_MODEL)],
        out_shape=[jax.ShapeDtypeStruct((bsz, t_len, D_MODEL), F32),
                   jax.ShapeDtypeStruct((bsz, t_len, D_MODEL), BF16)],
        compiler_params=_params("parallel", "parallel"),
        name="mix",
    )(h, gates, ya, yb, *weights)


def _ffn_kernel(hn_ref, h1_ref, halo_ref, wup_ref, cw_ref, cb_ref, wd_ref, gpost_ref,
                out_ref, halo_out_ref, halo_sc, *, tm, tf):
    @pl.when(pl.program_id(1) == 0)
    def _():
        halo_sc[...] = halo_ref[...]

    n_sub = 2 if tm % 512 == 0 else 1
    sub = tm // n_sub
    rws = [slice(q * sub, (q + 1) * sub) for q in range(n_sub)]
    row = lax.broadcasted_iota(jnp.int32, (sub, tf), 0)
    acc = [None] * n_sub
    for f in range(D_FF // tf):
        cols = slice(f * tf, (f + 1) * tf)
        hn = [hn_ref[r, :] for r in rws]
        ua = [_dot(x, wup_ref[:, cols]) for x in hn]
        ub = [_dot(x, wup_ref[:, D_FF + f * tf:D_FF + (f + 1) * tf]) for x in hn]
        tails = [halo_sc[:, cols]] + [x[sub - 8:, :] for x in ua]
        act = []
        for q in range(n_sub):
            prev1 = tails[q][7:8, :]
            prev2 = tails[q][6:7, :]
            u1 = jnp.where(row == 0, prev1, pltpu.roll(ua[q], 1, 0))
            u2 = jnp.where(row == 0, prev2, jnp.where(row == 1, prev1, pltpu.roll(ua[q], 2, 0)))
            conv = cw_ref[0:1, cols] * u2 + cw_ref[1:2, cols] * u1 + cw_ref[2:3, cols] * ua[q] + cb_ref[:, cols]
            act.append((_gelu_tanh(conv) * ub[q]).astype(BF16))
        halo_sc[:, cols] = tails[n_sub]
        part = [_dot(x, wd_ref[cols, :]) for x in act]
        acc = [part[q] if acc[q] is None else acc[q] + part[q] for q in range(n_sub)]
    halo_out_ref[...] = halo_sc[...]
    for q, r in enumerate(rws):
        out_ref[r, :] = h1_ref[r, :] + _rms(acc[q], gpost_ref[...])


def _ffn(hn, h1, halo_in, prm, tm, tf):
    bsz, t_len, _ = hn.shape
    n_t = t_len // tm
    row = lambda c: pl.BlockSpec((None, tm, c), lambda b, i: (b, i, 0))
    weights = [prm["w_up"], prm["conv_w"], prm["conv_b"], prm["w_down"], prm["g_ffn_post"]]
    return pl.pallas_call(
        functools.partial(_ffn_kernel, tm=tm, tf=tf),
        grid=(bsz, n_t),
        in_specs=[row(D_MODEL), row(D_MODEL), _resident(halo_in)] + [_resident(a) for a in weights],
        out_specs=[row(D_MODEL), pl.BlockSpec((None, None, 8, D_FF), lambda b, i: (b, i, 0, 0))],
        out_shape=[jax.ShapeDtypeStruct((bsz, t_len, D_MODEL), F32),
                   jax.ShapeDtypeStruct((bsz, n_t, 8, D_FF), F32)],
        scratch_shapes=[pltpu.VMEM((8, D_FF), F32)],
        compiler_params=_params("parallel", "arbitrary"),
        name="ffn",
    )(hn, h1, halo_in, *weights)


def _pick(t_len, options):
    for o in options:
        if t_len % o == 0:
            return o
    raise ValueError(f"no tile for sequence length {t_len}")


def _block(h, carry, prm, tab, n_pad=0):
    t_len = h.shape[1]
    keep = (jnp.arange(t_len) >= n_pad)[None, :, None]
    tm_proj = _pick(t_len, (512, 64))
    tm_rwkv = _pick(t_len, (256, 64))
    rows_s5 = _pick(t_len // S5_TOK, (512, 8))
    tm_mix = _pick(t_len, (1024, 64))
    tm_ffn = _pick(t_len, (512, 64))
    *rwkv_ops, u, gates, last_rows = _inproj(h, carry["prev_row"], prm, tm_proj)
    ya, rwkv_state = _rwkv(rwkv_ops, carry["rwkv_state"], prm, tm_rwkv)
    yb, s5_state = _s5(u, carry["s5_state"], tab, rows_s5)
    h1, hn2 = _mix(h, gates, ya, yb, prm, tm_mix)
    if n_pad:
        hn2 = jnp.where(keep, hn2, jnp.zeros_like(hn2))
    out, halo = _ffn(hn2, h1, carry["halo"], prm, tm_ffn, D_FF)
    if n_pad:
        out = jnp.where(keep, out, jnp.zeros_like(out))
    new_carry = {"prev_row": last_rows[0, -1], "rwkv_state": rwkv_state[0], "s5_state": s5_state[0],
                 "halo": halo[0, -1]}
    return out, new_carry


def kernel(x, meta_tokens, norm_mix_pre, norm_mix_post, w_in, rwkv_shift_mu, rwkv_w0, rwkv_w_up, rwkv_a0, rwkv_a_up, rwkv_g_up, rwkv_k_k, rwkv_k_a, rwkv_r_k, rwkv_ln_gain, rwkv_ln_bias, rwkv_w_branch, s5_lambda_re, s5_lambda_im, s5_log_dt, s5_b_re, s5_b_im, s5_c_re, s5_c_im, s5_d, s5_w_glu, w_out, norm_ffn_pre, norm_ffn_post, ffn_w_up, ffn_conv_w, ffn_conv_b, ffn_w_down):
    depth = w_in.shape[0]
    n_pairs = RWKV_WIDTH // LANES
    zeros_lora = jnp.zeros((W_LORA, RWKV_WIDTH), F32)

    h_meta = jnp.concatenate([jnp.zeros((META_PAD - N_META, D_MODEL), x.dtype), meta_tokens.astype(x.dtype)])[None]
    h_main = x
    for l in range(depth):
        vec = lambda a: a[l].reshape(1, -1).astype(F32)
        prm = {
            "g_pre": vec(norm_mix_pre), "g_post": vec(norm_mix_post),
            "g_ffn_pre": vec(norm_ffn_pre), "g_ffn_post": vec(norm_ffn_post),
            "w_in": w_in[l].astype(BF16),
            "mu": vec(rwkv_shift_mu), "w0": vec(rwkv_w0), "a0": vec(rwkv_a0),
            "wup": jnp.concatenate([rwkv_w_up[l].astype(F32), zeros_lora]),
            "aup": jnp.concatenate([zeros_lora, rwkv_a_up[l].astype(F32)]),
            "gup": rwkv_g_up[l].astype(F32),
            "k_k": vec(rwkv_k_k), "k_a": vec(rwkv_k_a), "r_k": vec(rwkv_r_k),
            "ln_g": vec(rwkv_ln_gain), "ln_b": vec(rwkv_ln_bias),
            "w_branch": rwkv_w_branch[l].astype(BF16), "w_glu": s5_w_glu[l].astype(BF16),
            "w_out": w_out[l].astype(BF16), "w_up": ffn_w_up[l].astype(BF16),
            "conv_w": ffn_conv_w[l].astype(F32), "conv_b": vec(ffn_conv_b),
            "w_down": ffn_w_down[l].astype(BF16),
        }
        tab = _s5_tables(s5_lambda_re[l], s5_lambda_im[l], s5_log_dt[l], s5_b_re[l],
                         s5_b_im[l], s5_c_re[l], s5_c_im[l], s5_d[l])
        zero_carry = {"prev_row": jnp.zeros((1, RWKV_PROJ), F32),
                      "rwkv_state": jnp.zeros((n_pairs, LANES, LANES), F32),
                      "s5_state": jnp.zeros((S5_WIDTH // LANES, 2, S5_BLOCK_STATE), F32),
                      "halo": jnp.zeros((8, D_FF), F32)}
        h_meta, carry = _block(h_meta, zero_carry, prm, tab, n_pad=META_PAD - N_META)
        h_main, _ = _block(h_main, carry, prm, tab)
    return h_main
```
